```python
import jax, jax.numpy as jnp
from jax import lax
import numpy as np

D_MODEL = 1024
BATCH = 8
SEQ = 8192
DEPTH = 1
DEC_BATCH = 8
DEC_SEQ = 64
PAST_LEN = 2048

CHUNK = 64
QBLOCK = 128
H_A = 8
DH_A = 64
W_A = H_A * DH_A
H_B = 4
DK_B = 128
DV_B = 256
QK_B = H_B * DK_B
V_B = H_B * DV_B
D_FF = 2816
CONV_W = 3
ROPE_BASE = 10000.0
EPS = 1e-6
IN_COLS = 3 * W_A + 2 * QK_B + 2 * V_B + 2 * D_MODEL
SPLIT_POINTS = (W_A, 2 * W_A, 3 * W_A, 3 * W_A + QK_B, 3 * W_A + 2 * QK_B, 3 * W_A + 2 * QK_B + V_B, 3 * W_A + 2 * QK_B + 2 * V_B, 3 * W_A + 2 * QK_B + 2 * V_B + D_MODEL)

kernel_name = 'stick_retention_convffn_stream'


def _rmsnorm(x, g):
    xf = x.astype(jnp.float32)
    y = xf * lax.rsqrt(jnp.mean(xf * xf, axis=-1, keepdims=True) + EPS)
    return (y * g.astype(jnp.float32)).astype(x.dtype)


def _split_heads(x, n_heads, head_dim):
    b, l, _ = x.shape
    return x.reshape(b, l, n_heads, head_dim).transpose(0, 2, 1, 3)


def _merge_heads(x):
    b, h, l, d = x.shape
    return x.transpose(0, 2, 1, 3).reshape(b, l, h * d)


def _rope(x, pos):
    half = x.shape[-1] // 2
    inv_freq = ROPE_BASE ** (-jnp.arange(half, dtype=jnp.float32) / half)
    ang = pos.astype(jnp.float32)[:, None] * inv_freq[None, :]
    cos, sin = jnp.cos(ang), jnp.sin(ang)
    x1, x2 = x[..., :half], x[..., half:]
    return jnp.concatenate([x1 * cos - x2 * sin, x1 * sin + x2 * cos], axis=-1)


def _stick_breaking(q, k, v, q_start):
    lq = q.shape[2]
    scale = q.shape[-1] ** -0.5
    outs = []
    for b0 in range(0, lq, QBLOCK):
        qb = q[:, :, b0:b0 + QBLOCK]
        nq = qb.shape[2]
        k_end = q_start + b0 + nq
        kb, vb = k[:, :, :k_end], v[:, :, :k_end]
        z = jnp.einsum('bhqd,bhkd->bhqk', qb, kb) * scale
        t_idx = q_start + b0 + jnp.arange(nq)
        mask = jnp.arange(k_end)[None, :] < t_idx[:, None]
        log_beta = jax.nn.log_sigmoid(z)
        log_keep = jnp.where(mask, log_beta - z, 0.0)
        log_a = log_beta + lax.cumsum(log_keep, axis=3, reverse=True) - log_keep
        a = jnp.where(mask, jnp.exp(log_a), 0.0)
        outs.append(jnp.einsum('bhqk,bhkd->bhqd', a, vb))
    return jnp.concatenate(outs, axis=2)


def _retention_chunk(s0, q, k, v, log_gamma):
    l = q.shape[2]
    idx = jnp.arange(l, dtype=jnp.float32)
    diff = idx[:, None] - idx[None, :]
    decay = jnp.where(diff >= 0, jnp.exp(jnp.maximum(diff, 0.0)[None] * log_gamma[:, None, None]), 0.0)
    scores = jnp.einsum('bhid,bhjd->bhij', q, k) * decay[None]
    q_decay = jnp.exp((idx + 1.0)[None, :] * log_gamma[:, None])[None, :, :, None]
    o = jnp.einsum('bhij,bhjv->bhiv', scores, v) + q_decay * jnp.einsum('bhid,bhdv->bhiv', q, s0)
    k_decay = jnp.exp((l - 1.0 - idx)[None, :] * log_gamma[:, None])[None, :, :, None]
    s1 = jnp.exp(l * log_gamma)[None, :, None, None] * s0 + jnp.einsum('bhjd,bhjv->bhdv', k * k_decay, v)
    return s1, o


def _retention_prompt(q, k, v, log_gamma):
    b, h, s, _ = q.shape
    nc = s // CHUNK

    def to_chunks(t):
        return t.reshape(b, h, nc, CHUNK, t.shape[-1]).transpose(2, 0, 1, 3, 4)

    def step(state, qkv):
        qc, kc, vc = qkv
        return _retention_chunk(state, qc, kc, vc, log_gamma)

    s0 = jnp.zeros((b, h, DK_B, DV_B), jnp.float32)
    s_fin, o = lax.scan(step, s0, (to_chunks(q), to_chunks(k), to_chunks(v)))
    o = o.transpose(1, 2, 0, 3, 4).reshape(b, h, s, DV_B)
    return s_fin, o


def _layer(x, pos, past_k, past_v, ret_state, conv_buf,
           g_mix, w_in, b_gate, w_pa, w_pb, w_o, g_ffn, w_a, w_b, w_conv, b_conv, w_down):
    f32 = jnp.float32
    bsz, l, _ = x.shape
    h = _rmsnorm(x, g_mix)
    proj = h @ w_in
    qa, ka, va, qb, kb, vb, gr, ga, gb = jnp.split(proj, list(SPLIT_POINTS), axis=-1)

    qa = _split_heads(qa, H_A, DH_A).astype(f32)
    ka = _split_heads(ka, H_A, DH_A).astype(f32)
    va = _split_heads(va, H_A, DH_A).astype(f32)
    if past_k is None:
        k_all, v_all, q_start = ka, va, 0
    else:
        k_all = jnp.concatenate([past_k.astype(f32), ka], axis=2)
        v_all = jnp.concatenate([past_v.astype(f32), va], axis=2)
        q_start = past_k.shape[2]
    o_a = _merge_heads(_stick_breaking(qa, k_all, v_all, q_start)).astype(x.dtype)

    log_gamma = jnp.log1p(-jnp.exp2(-5.0 - jnp.arange(H_B, dtype=f32)))
    qr = _rope(_split_heads(qb, H_B, DK_B).astype(f32), pos)
    kr = _rope(_split_heads(kb, H_B, DK_B).astype(f32), pos) * (DK_B ** -0.5)
    vr = _split_heads(vb, H_B, DV_B).astype(f32)
    if ret_state is None:
        s_new, o_b = _retention_prompt(qr, kr, vr, log_gamma)
    else:
        s_new, o_b = _retention_chunk(ret_state.astype(f32), qr, kr, vr, log_gamma)
    o_b = o_b * lax.rsqrt(jnp.mean(o_b * o_b, axis=-1, keepdims=True) + EPS)
    o_b = _merge_heads(o_b).astype(x.dtype) * jax.nn.silu(gr)

    gate_a = jax.nn.sigmoid(ga + b_gate[0])
    gate_b = jax.nn.sigmoid(gb + b_gate[1])
    mix = gate_a * (o_a @ w_pa) + gate_b * (o_b @ w_pb)
    x = x + mix @ w_o

    hn = _rmsnorm(x, g_ffn)
    a = hn @ w_a
    if conv_buf is None:
        conv_buf = jnp.zeros((bsz, CONV_W - 1, D_FF), a.dtype)
    a_ext = jnp.concatenate([conv_buf.astype(a.dtype), a], axis=1)
    conv = sum(w_conv[j] * a_ext[:, j:j + l] for j in range(CONV_W)) + b_conv
    hid = jax.nn.silu(conv) * (hn @ w_b)
    x = x + hid @ w_down
    new_buf = a_ext[:, l:]
    return x, ka, va, s_new, new_buf


def setup_inputs(seed: int = 0) -> dict:
    key = jax.random.key(seed)
    ks = jax.random.split(key, 20)
    nrm = jax.random.normal
    f32 = jnp.float32
    return {
        'x_prompt': nrm(ks[0], (BATCH, SEQ, D_MODEL), f32),
        'x_sample': nrm(ks[1], (DEC_BATCH, DEC_SEQ, D_MODEL), f32),
        'cache_k_sb': nrm(ks[2], (DEPTH, DEC_BATCH, H_A, PAST_LEN, DH_A), f32),
        'cache_v_sb': nrm(ks[3], (DEPTH, DEC_BATCH, H_A, PAST_LEN, DH_A), f32),
        'state_ret': nrm(ks[4], (DEPTH, DEC_BATCH, H_B, DK_B, DV_B), f32),
        'state_conv': nrm(ks[5], (DEPTH, DEC_BATCH, CONV_W - 1, D_FF), f32),
        'g_mix': 1.0 + 0.1 * nrm(ks[6], (DEPTH, D_MODEL), f32),
        'w_in': nrm(ks[7], (DEPTH, D_MODEL, IN_COLS), f32) * D_MODEL ** -0.5,
        'b_gate': 0.1 * nrm(ks[8], (DEPTH, 2, D_MODEL), f32),
        'w_pa': nrm(ks[9], (DEPTH, W_A, D_MODEL), f32) * W_A ** -0.5,
        'w_pb': nrm(ks[10], (DEPTH, V_B, D_MODEL), f32) * V_B ** -0.5,
        'w_o': nrm(ks[11], (DEPTH, D_MODEL, D_MODEL), f32) * D_MODEL ** -0.5,
        'g_ffn': 1.0 + 0.1 * nrm(ks[12], (DEPTH, D_MODEL), f32),
        'w_a': nrm(ks[13], (DEPTH, D_MODEL, D_FF), f32) * D_MODEL ** -0.5,
        'w_b': nrm(ks[14], (DEPTH, D_MODEL, D_FF), f32) * D_MODEL ** -0.5,
        'w_conv': nrm(ks[15], (DEPTH, CONV_W, D_FF), f32) * CONV_W ** -0.5,
        'b_conv': 0.1 * nrm(ks[16], (DEPTH, D_FF), f32),
        'w_down': nrm(ks[17], (DEPTH, D_FF, D_MODEL), f32) * D_FF ** -0.5,
        'g_final': 1.0 + 0.1 * nrm(ks[18], (D_MODEL,), f32),
    }


def reference(x_prompt, x_sample, cache_k_sb, cache_v_sb, state_ret, state_conv,
              g_mix, w_in, b_gate, w_pa, w_pb, w_o, g_ffn, w_a, w_b, w_conv, b_conv, w_down, g_final):
    pos_p = jnp.arange(x_prompt.shape[1], dtype=jnp.int32)
    pos_s = cache_k_sb.shape[3] + jnp.arange(x_sample.shape[1], dtype=jnp.int32)
    hp, hs = x_prompt, x_sample
    kp, vp, sp, cp = [], [], [], []
    ksl, vsl, ssl, csl = [], [], [], []
    for d in range(DEPTH):
        wts = (g_mix[d], w_in[d], b_gate[d], w_pa[d], w_pb[d], w_o[d],
               g_ffn[d], w_a[d], w_b[d], w_conv[d], b_conv[d], w_down[d])
        hp, k1, v1, s1, c1 = _layer(hp, pos_p, None, None, None, None, *wts)
        hs, k2, v2, s2, c2 = _layer(hs, pos_s, cache_k_sb[d], cache_v_sb[d], state_ret[d], state_conv[d], *wts)
        kp.append(k1); vp.append(v1); sp.append(s1); cp.append(c1)
        ksl.append(k2); vsl.append(v2); ssl.append(s2); csl.append(c2)
    y_prompt = _rmsnorm(hp, g_final)
    y_sample = _rmsnorm(hs, g_final)
    new_k_sb_prompt = jnp.stack(kp)
    new_v_sb_prompt = jnp.stack(vp)
    new_ret_prompt = jnp.stack(sp)
    new_conv_prompt = jnp.stack(cp)
    new_k_sb_sample = jnp.stack(ksl)
    new_v_sb_sample = jnp.stack(vsl)
    new_ret_sample = jnp.stack(ssl)
    new_conv_sample = jnp.stack(csl)
    return (y_prompt, y_sample, new_k_sb_prompt, new_v_sb_prompt, new_ret_prompt, new_conv_prompt, new_k_sb_sample, new_v_sb_sample, new_ret_sample, new_conv_sample)
```

```python
import functools

import jax
import jax.numpy as jnp
from jax import lax
from jax.experimental import pallas as pl
from jax.experimental.pallas import tpu as pltpu

F32 = jnp.float32
BF16 = jnp.bfloat16

D_MODEL = 1024
H_A, DH_A = 8, 64
W_A = H_A * DH_A
H_B, DK_B, DV_B = 4, 128, 256
QK_B = H_B * DK_B
V_B = H_B * DV_B
D_FF = 2816
CONV_W = 3
ROPE_BASE = 10000.0
EPS = 1e-6

C_QA, C_KA, C_VA = 0, W_A, 2 * W_A
C_QB = 3 * W_A
C_KB = C_QB + QK_B
C_VB = C_KB + QK_B
C_GR = C_VB + V_B
C_GA = C_GR + V_B
C_GB = C_GA + D_MODEL
IN_COLS = C_GB + D_MODEL

VMEM_LIMIT_BYTES = 56 * 1024 * 1024

STICK_DEAD_LOG = -110.0


def _dot(a, b):
    return jnp.dot(a, b, preferred_element_type=F32)


def _dot_nt(a, b):
    return lax.dot_general(a, b, (((1,), (1,)), ((), ())), preferred_element_type=F32)


def _rmsnorm(x, g):
    return x * lax.rsqrt(jnp.mean(x * x, axis=-1, keepdims=True) + EPS) * g


def _sigmoid(x):
    return 1.0 / (1.0 + jnp.exp(-x))


def _resident(shape):
    nd = len(shape)
    return pl.BlockSpec(shape, lambda *_: (0,) * nd, pipeline_mode=pl.Buffered(1))


def _in_proj_kernel(x_ref, g_ref, w_ref, cos_ref, sin_ref,
                    qa_ref, ka_ref, va_ref, qb_ref, kb_ref, vb_ref, gr_ref, ga_ref, gb_ref):
    h = _rmsnorm(x_ref[0], g_ref[...]).astype(BF16)

    def proj(c0, width):
        return _dot(h, w_ref[:, c0:c0 + width])

    qa_ref[0] = (proj(C_QA, W_A) * (-(DH_A ** -0.5))).astype(BF16)
    ka = proj(C_KA, W_A)
    va = proj(C_VA, W_A)
    for hh in range(H_A):
        ka_ref[0, hh] = ka[:, hh * DH_A:(hh + 1) * DH_A]
        va_ref[0, hh] = va[:, hh * DH_A:(hh + 1) * DH_A]

    cos = cos_ref[...]
    sin = sin_ref[...]
    qb = proj(C_QB, QK_B)
    kb = proj(C_KB, QK_B)
    for hh in range(H_B):
        sl = slice(hh * DK_B, (hh + 1) * DK_B)
        q = qb[:, sl]
        k = kb[:, sl]
        qb_ref[0, :, sl] = (q * cos + pltpu.roll(q, DK_B // 2, axis=1) * sin).astype(BF16)
        kr = (k * cos + pltpu.roll(k, DK_B // 2, axis=1) * sin) * (DK_B ** -0.5)
        kb_ref[0, :, sl] = kr.astype(BF16)

    vb_ref[0] = proj(C_VB, V_B).astype(BF16)
    gr_ref[0] = proj(C_GR, V_B).astype(BF16)
    ga_ref[0] = proj(C_GA, D_MODEL).astype(BF16)
    gb_ref[0] = proj(C_GB, D_MODEL).astype(BF16)


def _in_proj(x, g, w_in, cos, sin, tm):
    nb, l, _ = x.shape
    nt = l // tm
    tok = lambda width: pl.BlockSpec((1, tm, width), lambda b, i: (b, i, 0))
    heads = pl.BlockSpec((1, H_A, tm, DH_A), lambda b, i: (b, 0, i, 0))
    tab = pl.BlockSpec((tm, DK_B), lambda b, i: (i, 0))
    sds = jax.ShapeDtypeStruct
    return pl.pallas_call(
        _in_proj_kernel,
        grid=(nb, nt),
        in_specs=[tok(D_MODEL), _resident((1, D_MODEL)), _resident((D_MODEL, IN_COLS)), tab, tab],
        out_specs=[tok(W_A), heads, heads, tok(QK_B), tok(QK_B), tok(V_B), tok(V_B),
                   tok(D_MODEL), tok(D_MODEL)],
        out_shape=[sds((nb, l, W_A), BF16),
                   sds((nb, H_A, l, DH_A), F32), sds((nb, H_A, l, DH_A), F32),
                   sds((nb, l, QK_B), BF16), sds((nb, l, QK_B), BF16),
                   sds((nb, l, V_B), BF16), sds((nb, l, V_B), BF16),
                   sds((nb, l, D_MODEL), BF16), sds((nb, l, D_MODEL), BF16)],
        compiler_params=pltpu.CompilerParams(
            dimension_semantics=("arbitrary", "arbitrary"), vmem_limit_bytes=VMEM_LIMIT_BYTES),
        name="in_proj",
    )(x, g, w_in, cos, sin)


def _upper_ones(n):
    j = lax.broadcasted_iota(jnp.int32, (n, n), 0)
    s = lax.broadcasted_iota(jnp.int32, (n, n), 1)
    return jnp.where(j > s, 1.0, 0.0).astype(BF16)


def _stick_block(qn, k, v, tri, carry, diagonal):
    nz = _dot_nt(qn, k)
    soft = jnp.log(1.0 + jnp.exp(jnp.minimum(nz, -nz)))
    log_keep = jnp.minimum(nz, 0.0) - soft
    log_beta = log_keep - nz
    if diagonal:
        nq, nk = nz.shape
        t = lax.broadcasted_iota(jnp.int32, (nq, nk), 0)
        s = lax.broadcasted_iota(jnp.int32, (nq, nk), 1)
        mask = s < t
        log_keep = jnp.where(mask, log_keep, 0.0)
    later = _dot(log_keep.astype(BF16), tri)
    log_a = log_beta + later
    if carry is not None:
        log_a = log_a + carry
    a = jnp.exp(log_a)
    if diagonal:
        a = jnp.where(mask, a, 0.0)
    out = _dot(a.astype(BF16), v)
    total = later[:, 0:1] + log_keep[:, 0:1]
    return out, total


def _stick_walk(qn, load_kv, n_blocks, tri, acc_ref, r_ref):
    def cond(c):
        kb, alive = c
        return jnp.logical_and(kb >= 0, alive > 0)

    def body(c):
        kb, _ = c
        k, v = load_kv(kb)
        r = r_ref[...]
        out, total = _stick_block(qn, k, v, tri, r, diagonal=False)
        acc_ref[...] += out
        r = r + total
        r_ref[...] = r
        return kb - 1, (jnp.max(r) > STICK_DEAD_LOG).astype(jnp.int32)

    alive0 = (jnp.max(r_ref[...]) > STICK_DEAD_LOG).astype(jnp.int32)
    lax.while_loop(cond, body, (n_blocks - 1, alive0))


def _stick_prompt_kernel(q_ref, k_ref, v_ref, o_ref, acc_ref, r_ref, *, blk, heads):
    qi = pl.program_id(2)
    tri = _upper_ones(blk)
    for hh in range(heads):
        qn = q_ref[0, :, hh * DH_A:(hh + 1) * DH_A]

        def load_kv(kb, hh=hh):
            start = pl.multiple_of(kb * blk, blk)
            return (k_ref[0, hh, pl.ds(start, blk), :].astype(BF16),
                    v_ref[0, hh, pl.ds(start, blk), :].astype(BF16))

        k, v = load_kv(qi)
        out, total = _stick_block(qn, k, v, tri, None, diagonal=True)
        acc_ref[...] = out
        r_ref[...] = total
        _stick_walk(qn, load_kv, qi, tri, acc_ref, r_ref)
        o_ref[0, :, hh * DH_A:(hh + 1) * DH_A] = acc_ref[...].astype(BF16)


def _stick_prompt(qn, k, v, blk, heads=2):
    nb, l, _ = qn.shape
    kv = pl.BlockSpec((1, heads, l, DH_A), lambda b, h, i: (b, h, 0, 0))
    qo = pl.BlockSpec((1, blk, heads * DH_A), lambda b, h, i: (b, i, h))
    return pl.pallas_call(
        functools.partial(_stick_prompt_kernel, blk=blk, heads=heads),
        grid=(nb, H_A // heads, l // blk),
        in_specs=[qo, kv, kv],
        out_specs=qo,
        out_shape=jax.ShapeDtypeStruct((nb, l, W_A), BF16),
        scratch_shapes=[pltpu.VMEM((blk, DH_A), F32), pltpu.VMEM((blk, 1), F32)],
        compiler_params=pltpu.CompilerParams(
            dimension_semantics=("arbitrary", "arbitrary", "arbitrary"),
            vmem_limit_bytes=VMEM_LIMIT_BYTES),
        name="stick_prompt",
    )(qn, k, v)


def _stick_sample_kernel(q_ref, kn_ref, vn_ref, kc_ref, vc_ref, o_ref, acc_ref, r_ref,
                         *, blk, heads):
    lq = q_ref.shape[1]
    n_cache = kc_ref.shape[2] // blk
    tri_new = _upper_ones(lq)
    tri = _upper_ones(blk)
    for hh in range(heads):
        qn = q_ref[0, :, hh * DH_A:(hh + 1) * DH_A]

        def load_kv(kb, hh=hh):
            start = pl.multiple_of(kb * blk, blk)
            return (kc_ref[0, hh, pl.ds(start, blk), :].astype(BF16),
                    vc_ref[0, hh, pl.ds(start, blk), :].astype(BF16))

        out, total = _stick_block(qn, kn_ref[0, hh].astype(BF16), vn_ref[0, hh].astype(BF16),
                                  tri_new, None, diagonal=True)
        acc_ref[...] = out
        r_ref[...] = total
        _stick_walk(qn, load_kv, n_cache, tri, acc_ref, r_ref)
        o_ref[0, :, hh * DH_A:(hh + 1) * DH_A] = acc_ref[...].astype(BF16)


def _stick_sample(qn, k_new, v_new, k_cache, v_cache, blk, heads=2):
    nb, lq, _ = qn.shape
    past = k_cache.shape[2]
    new = pl.BlockSpec((1, heads, lq, DH_A), lambda b, h: (b, h, 0, 0))
    old = pl.BlockSpec((1, heads, past, DH_A), lambda b, h: (b, h, 0, 0))
    qo = pl.BlockSpec((1, lq, heads * DH_A), lambda b, h: (b, 0, h))
    return pl.pallas_call(
        functools.partial(_stick_sample_kernel, blk=blk, heads=heads),
        grid=(nb, H_A // heads),
        in_specs=[qo, new, new, old, old],
        out_specs=qo,
        out_shape=jax.ShapeDtypeStruct((nb, lq, W_A), BF16),
        scratch_shapes=[pltpu.VMEM((lq, DH_A), F32), pltpu.VMEM((lq, 1), F32)],
        compiler_params=pltpu.CompilerParams(
            dimension_semantics=("arbitrary", "arbitrary"), vmem_limit_bytes=VMEM_LIMIT_BYTES),
        name="stick_sample",
    )(qn, k_new, v_new, k_cache, v_cache)


def _retention_kernel(lg_ref, q_ref, k_ref, v_ref, gr_ref, s0_ref, o_ref, s1_ref,
                      state_ref, decay_ref, *, chunk):
    hh = pl.program_id(1)
    c = pl.program_id(2)
    lg = lg_ref[hh]

    @pl.when(c == 0)
    def _():
        state_ref[...] = s0_ref[0, 0]
        i = lax.broadcasted_iota(jnp.int32, (chunk, chunk), 0)
        j = lax.broadcasted_iota(jnp.int32, (chunk, chunk), 1)
        diff = (i - j).astype(F32)
        decay_ref[...] = jnp.where(diff >= 0, jnp.exp(jnp.maximum(diff, 0.0) * lg), 0.0)

    q = q_ref[0]
    k = k_ref[0]
    v = v_ref[0]
    state = state_ref[...]
    idx = lax.broadcasted_iota(jnp.int32, (chunk, 1), 0).astype(F32)

    scores = _dot_nt(q, k) * decay_ref[...]
    o = _dot(scores.astype(BF16), v) + jnp.exp((idx + 1.0) * lg) * _dot(q, state.astype(BF16))
    k_dec = (k.astype(F32) * jnp.exp((chunk - 1.0 - idx) * lg)).T.astype(BF16)
    state_ref[...] = jnp.exp(chunk * lg) * state + _dot(k_dec, v)

    o = o * lax.rsqrt(jnp.mean(o * o, axis=-1, keepdims=True) + EPS)
    g = gr_ref[0].astype(F32)
    o_ref[0] = (o * (g * _sigmoid(g))).astype(BF16)

    @pl.when(c == pl.num_programs(2) - 1)
    def _():
        s1_ref[0, 0] = state_ref[...]


def _retention(log_gamma, q, k, v, gr, s0, chunk):
    nb, l, _ = q.shape
    qk = pl.BlockSpec((1, chunk, DK_B), lambda b, h, c: (b, c, h))
    vv = pl.BlockSpec((1, chunk, DV_B), lambda b, h, c: (b, c, h))
    st = pl.BlockSpec((1, 1, DK_B, DV_B), lambda b, h, c: (b, h, 0, 0))
    return pl.pallas_call(
        functools.partial(_retention_kernel, chunk=chunk),
        grid=(nb, H_B, l // chunk),
        in_specs=[pl.BlockSpec(memory_space=pltpu.SMEM), qk, qk, vv, vv, st],
        out_specs=[vv, st],
        out_shape=[jax.ShapeDtypeStruct((nb, l, V_B), BF16),
                   jax.ShapeDtypeStruct((nb, H_B, DK_B, DV_B), F32)],
        scratch_shapes=[pltpu.VMEM((DK_B, DV_B), F32), pltpu.VMEM((chunk, chunk), F32)],
        compiler_params=pltpu.CompilerParams(
            dimension_semantics=("arbitrary", "arbitrary", "arbitrary"),
            vmem_limit_bytes=VMEM_LIMIT_BYTES),
        name="retention",
    )(log_gamma, q, k, v, gr, s0)


def _out_ffn_kernel(x_ref, oa_ref, ob_ref, ga_ref, gb_ref, cbuf_ref, bg_ref,
                    wpa_ref, wpb_ref, wo_ref, gffn_ref, wa_ref, wb_ref, wc_ref, bc_ref,
                    wd_ref, gfin_ref, y_ref, cnew_ref, carry_ref, *, final):
    tm = x_ref.shape[1]

    @pl.when(pl.program_id(1) == 0)
    def _():
        carry_ref[...] = cbuf_ref[0]

    gate_a = _sigmoid(ga_ref[0].astype(F32) + bg_ref[0:1, :])
    gate_b = _sigmoid(gb_ref[0].astype(F32) + bg_ref[1:2, :])
    mix = gate_a * _dot(oa_ref[0], wpa_ref[...]) + gate_b * _dot(ob_ref[0], wpb_ref[...])
    x1 = x_ref[0] + _dot(mix.astype(BF16), wo_ref[...])

    hn = _rmsnorm(x1, gffn_ref[...]).astype(BF16)
    a = _dot(hn, wa_ref[...])
    up = _dot(hn, wb_ref[...])
    prev2 = carry_ref[0:1, :]
    prev1 = carry_ref[1:2, :]
    row = lax.broadcasted_iota(jnp.int32, (tm, 1), 0)
    a_m1 = jnp.where(row == 0, prev1, pltpu.roll(a, 1, axis=0))
    a_m2 = jnp.where(row == 0, prev2, jnp.where(row == 1, prev1, pltpu.roll(a, 2, axis=0)))
    conv = wc_ref[0:1, :] * a_m2 + wc_ref[1:2, :] * a_m1 + wc_ref[2:3, :] * a + bc_ref[...]
    hid = conv * _sigmoid(conv) * up
    x2 = x1 + _dot(hid.astype(BF16), wd_ref[...])

    y_ref[0] = _rmsnorm(x2, gfin_ref[...]) if final else x2
    tail = a[tm - (CONV_W - 1):, :]
    carry_ref[...] = tail
    cnew_ref[0] = tail


def _out_ffn(x, oa, ob, ga, gb, conv_buf, wts, g_final, tm, final):
    b_gate, w_pa, w_pb, w_o, g_ffn, w_a, w_b, w_conv, b_conv, w_down = wts
    nb, l, _ = x.shape
    tok = lambda width: pl.BlockSpec((1, tm, width), lambda b, i: (b, i, 0))
    cb = pl.BlockSpec((1, CONV_W - 1, D_FF), lambda b, i: (b, 0, 0))
    consts = [b_gate, w_pa, w_pb, w_o, g_ffn, w_a, w_b, w_conv, b_conv, w_down, g_final]
    return pl.pallas_call(
        functools.partial(_out_ffn_kernel, final=final),
        grid=(nb, l // tm),
        in_specs=[tok(D_MODEL), tok(W_A), tok(V_B), tok(D_MODEL), tok(D_MODEL), cb]
                 + [_resident(c.shape) for c in consts],
        out_specs=[tok(D_MODEL), cb],
        out_shape=[jax.ShapeDtypeStruct((nb, l, D_MODEL), F32),
                   jax.ShapeDtypeStruct((nb, CONV_W - 1, D_FF), F32)],
        scratch_shapes=[pltpu.VMEM((CONV_W - 1, D_FF), F32)],
        compiler_params=pltpu.CompilerParams(
            dimension_semantics=("arbitrary", "arbitrary"), vmem_limit_bytes=VMEM_LIMIT_BYTES),
        name="out_ffn",
    )(x, oa, ob, ga, gb, conv_buf, *consts)


def _rope_tables(pos):
    half = DK_B // 2
    inv_freq = ROPE_BASE ** (-jnp.arange(half, dtype=F32) / half)
    ang = pos.astype(F32)[:, None] * inv_freq[None, :]
    cos, sin = jnp.cos(ang), jnp.sin(ang)
    return jnp.concatenate([cos, cos], axis=-1), jnp.concatenate([-sin, sin], axis=-1)


def _layer(x, pos, past_k, past_v, ret_state, conv_buf, wts, g_final, final, tiles):
    g_mix, w_in, *rest = wts
    nb = x.shape[0]
    cos, sin = _rope_tables(pos)
    qa, ka, va, qb, kb, vb, gr, ga, gb = _in_proj(x, g_mix, w_in, cos, sin, tiles["in"])

    if past_k is None:
        o_a = _stick_prompt(qa, ka, va, tiles["stick"])
    else:
        o_a = _stick_sample(qa, ka, va, past_k, past_v, tiles["stick"])

    log_gamma = jnp.log1p(-jnp.exp2(-5.0 - jnp.arange(H_B, dtype=F32)))
    if ret_state is None:
        ret_state = jnp.zeros((nb, H_B, DK_B, DV_B), F32)
    o_b, s_new = _retention(log_gamma, qb, kb, vb, gr, ret_state, tiles["ret"])

    if conv_buf is None:
        conv_buf = jnp.zeros((nb, CONV_W - 1, D_FF), F32)
    y, c_new = _out_ffn(x, o_a, o_b, ga, gb, conv_buf, rest, g_final, tiles["out"], final)
    return y, ka, va, s_new, c_new


PROMPT_TILES = {"in": 512, "stick": 256, "ret": 256, "out": 256}


def kernel(x_prompt, x_sample, cache_k_sb, cache_v_sb, state_ret, state_conv, g_mix, w_in, b_gate,
           w_pa, w_pb, w_o, g_ffn, w_a, w_b, w_conv, b_conv, w_down, g_final):
    depth = w_in.shape[0]
    past = cache_k_sb.shape[3]
    dec = x_sample.shape[1]
    pos_p = jnp.arange(x_prompt.shape[1], dtype=jnp.int32)
    pos_s = past + jnp.arange(dec, dtype=jnp.int32)
    sample_tiles = {"in": dec, "stick": 256, "ret": dec, "out": dec}
    g_fin = g_final.reshape(1, D_MODEL)

    hp, hs = x_prompt, x_sample
    outs = [[] for _ in range(8)]
    for d in range(depth):
        wts = (g_mix[d].reshape(1, D_MODEL), w_in[d].astype(BF16), b_gate[d],
               w_pa[d].astype(BF16), w_pb[d].astype(BF16), w_o[d].astype(BF16),
               g_ffn[d].reshape(1, D_MODEL), w_a[d].astype(BF16), w_b[d].astype(BF16),
               w_conv[d], b_conv[d].reshape(1, D_FF), w_down[d].astype(BF16))
        final = d == depth - 1
        hp, k1, v1, s1, c1 = _layer(hp, pos_p, None, None, None, None, wts, g_fin, final,
                                    PROMPT_TILES)
        hs, k2, v2, s2, c2 = _layer(hs, pos_s, cache_k_sb[d], cache_v_sb[d], state_ret[d],
                                    state_conv[d], wts, g_fin, final, sample_tiles)
        for lst, val in zip(outs, (k1, v1, s1, c1, k2, v2, s2, c2)):
            lst.append(val)
    stacked = [jnp.stack(o) for o in outs]
    return (hp, hs, *stacked)
```

```python
import functools

import jax
import jax.numpy as jnp
from jax import lax
from jax.experimental import pallas as pl
from jax.experimental.pallas import tpu as pltpu

F32 = jnp.float32
BF16 = jnp.bfloat16

D_MODEL = 1024
H_A, DH_A = 8, 64
W_A = H_A * DH_A
H_B, DK_B, DV_B = 4, 128, 256
QK_B = H_B * DK_B
V_B = H_B * DV_B
D_FF = 2816
CONV_W = 3
ROPE_BASE = 10000.0
EPS = 1e-6
LANES = 128
PAIR = LANES // DH_A

C_QA, C_KA, C_VA = 0, W_A, 2 * W_A
C_QB = 3 * W_A
C_KB = C_QB + QK_B
C_VB = C_KB + QK_B
C_GR = C_VB + V_B
C_GA = C_GR + V_B
C_GB = C_GA + D_MODEL
IN_COLS = C_GB + D_MODEL

VMEM_LIMIT_BYTES = 56 * 1024 * 1024

STICK_DEAD_LOG = -110.0
STICK_NO_KEYS = -1e30


def _dot(a, b):
    return jnp.dot(a, b, preferred_element_type=F32)


def _dot_nt(a, b):
    return lax.dot_general(a, b, (((1,), (1,)), ((), ())), preferred_element_type=F32)


def _rmsnorm(x, g):
    return x * lax.rsqrt(jnp.mean(x * x, axis=-1, keepdims=True) + EPS) * g


def _sigmoid(x):
    return 1.0 / (1.0 + jnp.exp(-x))


def _resident(shape):
    nd = len(shape)
    return pl.BlockSpec(shape, lambda *_: (0,) * nd, pipeline_mode=pl.Buffered(1))


def _in_proj_kernel(x_ref, g_ref, w_ref, cos_ref, sin_ref,
                    qa_ref, ka_ref, va_ref, qb_ref, kb_ref, vb_ref, gr_ref, ga_ref, gb_ref,
                    *kv16_refs):
    h = _rmsnorm(x_ref[0], g_ref[...]).astype(BF16)

    def proj(c0, width):
        return _dot(h, w_ref[:, c0:c0 + width])

    qa_ref[0] = (proj(C_QA, W_A) * (-(DH_A ** -0.5))).astype(BF16)
    ka = proj(C_KA, W_A)
    va = proj(C_VA, W_A)
    for hh in range(H_A):
        ka_ref[0, hh] = ka[:, hh * DH_A:(hh + 1) * DH_A]
        va_ref[0, hh] = va[:, hh * DH_A:(hh + 1) * DH_A]
    if kv16_refs:
        kv16_refs[0][0] = ka.astype(BF16)
        kv16_refs[1][0] = va.astype(BF16)

    cos = cos_ref[...]
    sin = sin_ref[...]
    qb = proj(C_QB, QK_B)
    kb = proj(C_KB, QK_B)
    for hh in range(H_B):
        sl = slice(hh * DK_B, (hh + 1) * DK_B)
        q = qb[:, sl]
        k = kb[:, sl]
        qb_ref[0, :, sl] = (q * cos + pltpu.roll(q, DK_B // 2, axis=1) * sin).astype(BF16)
        kr = (k * cos + pltpu.roll(k, DK_B // 2, axis=1) * sin) * (DK_B ** -0.5)
        kb_ref[0, :, sl] = kr.astype(BF16)

    vb_ref[0] = proj(C_VB, V_B).astype(BF16)
    gr_ref[0] = proj(C_GR, V_B).astype(BF16)
    ga_ref[0] = proj(C_GA, D_MODEL).astype(BF16)
    gb_ref[0] = proj(C_GB, D_MODEL).astype(BF16)


def _in_proj(x, g, w_in, cos, sin, tm, token_major_kv):
    nb, l, _ = x.shape
    nt = l // tm
    tok = lambda width: pl.BlockSpec((1, tm, width), lambda b, i: (b, i, 0))
    heads = pl.BlockSpec((1, H_A, tm, DH_A), lambda b, i: (b, 0, i, 0))
    tab = pl.BlockSpec((tm, DK_B), lambda b, i: (i, 0))
    sds = jax.ShapeDtypeStruct
    out_specs = [tok(W_A), heads, heads, tok(QK_B), tok(QK_B), tok(V_B), tok(V_B),
                 tok(D_MODEL), tok(D_MODEL)]
    out_shape = [sds((nb, l, W_A), BF16),
                 sds((nb, H_A, l, DH_A), F32), sds((nb, H_A, l, DH_A), F32),
                 sds((nb, l, QK_B), BF16), sds((nb, l, QK_B), BF16),
                 sds((nb, l, V_B), BF16), sds((nb, l, V_B), BF16),
                 sds((nb, l, D_MODEL), BF16), sds((nb, l, D_MODEL), BF16)]
    if token_major_kv:
        out_specs += [tok(W_A), tok(W_A)]
        out_shape += [sds((nb, l, W_A), BF16), sds((nb, l, W_A), BF16)]
    return pl.pallas_call(
        _in_proj_kernel,
        grid=(nb, nt),
        in_specs=[tok(D_MODEL), _resident((1, D_MODEL)), _resident((D_MODEL, IN_COLS)), tab, tab],
        out_specs=out_specs,
        out_shape=out_shape,
        compiler_params=pltpu.CompilerParams(
            dimension_semantics=("arbitrary", "arbitrary"), vmem_limit_bytes=VMEM_LIMIT_BYTES),
        name="in_proj",
    )(x, g, w_in, cos, sin)


def _upper_ones(n):
    j = lax.broadcasted_iota(jnp.int32, (n, n), 0)
    s = lax.broadcasted_iota(jnp.int32, (n, n), 1)
    return jnp.where(j > s, 1.0, 0.0).astype(BF16)


def _stick_block(qn, k, v, tri, carry, diagonal):
    nz = _dot_nt(qn, k)
    soft = jnp.log(1.0 + jnp.exp(jnp.minimum(nz, -nz)))
    log_keep = jnp.minimum(nz, 0.0) - soft
    log_beta = log_keep - nz
    if diagonal:
        nq, nk = nz.shape
        t = lax.broadcasted_iota(jnp.int32, (nq, nk), 0)
        s = lax.broadcasted_iota(jnp.int32, (nq, nk), 1)
        mask = s < t
        log_keep = jnp.where(mask, log_keep, 0.0)
    later = _dot(log_keep.astype(BF16), tri)
    log_a = log_beta + later
    if carry is not None:
        log_a = log_a + carry
    a = jnp.exp(log_a)
    if diagonal:
        a = jnp.where(mask, a, 0.0)
    out = _dot(a.astype(BF16), v)
    total = later[:, 0:1] + log_keep[:, 0:1]
    return out, total


def _any_row_alive(r_ref):
    m = r_ref[0]
    for c in range(1, r_ref.shape[0]):
        m = jnp.maximum(m, r_ref[c])
    return (jnp.max(m) > STICK_DEAD_LOG).astype(jnp.int32)


def _stick_prompt_kernel(q_ref, k_ref, v_ref, o_ref, qm_ref, tri_ref, acc_ref, r_ref,
                         *, blk, group):
    step = pl.program_id(2)
    tri_ref[...] = _upper_ones(blk)
    lane = lax.broadcasted_iota(jnp.int32, (blk, LANES), 1)
    first = lane < DH_A
    for g in range(group):
        qp = q_ref[0, g * blk:(g + 1) * blk, :]
        qm_ref[PAIR * g] = jnp.where(first, qp, jnp.zeros_like(qp))
        qm_ref[PAIR * g + 1] = jnp.where(first, jnp.zeros_like(qp), qp)

    def load_kv(kb):
        start = pl.multiple_of(kb * blk, blk)
        return k_ref[0, pl.ds(start, blk), :], v_ref[0, pl.ds(start, blk), :]

    for g in range(group):
        k, v = load_kv(step * group + g)
        outs = []
        for hh in range(PAIR):
            c = PAIR * g + hh
            out, total = _stick_block(qm_ref[c], k, v, tri_ref[...], None, diagonal=True)
            outs.append(out)
            r_ref[c] = total
        acc_ref[g] = jnp.where(first, outs[0], outs[1])

    def cond(carry):
        dist, alive = carry
        return jnp.logical_and(dist <= step * group + (group - 1), alive > 0)

    def body(carry):
        dist, _ = carry
        for g in range(group):
            kb = step * group + g - dist
            has_keys = kb >= 0
            k, v = load_kv(jnp.maximum(kb, 0))
            outs = []
            for hh in range(PAIR):
                c = PAIR * g + hh
                r = jnp.where(has_keys, r_ref[c], STICK_NO_KEYS)
                out, total = _stick_block(qm_ref[c], k, v, tri_ref[...], r, diagonal=False)
                outs.append(out)
                r_ref[c] = r + total
            acc_ref[g] += jnp.where(first, outs[0], outs[1])
        return dist + 1, _any_row_alive(r_ref)

    lax.while_loop(cond, body, (jnp.int32(1), _any_row_alive(r_ref)))
    for g in range(group):
        o_ref[0, g * blk:(g + 1) * blk, :] = acc_ref[g].astype(BF16)


def _stick_prompt(qn, k, v, blk, group):
    nb, l, _ = qn.shape
    rows = blk * group
    kv = pl.BlockSpec((1, l, LANES), lambda b, p, i: (b, 0, p))
    qo = pl.BlockSpec((1, rows, LANES), lambda b, p, i: (b, i, p))
    return pl.pallas_call(
        functools.partial(_stick_prompt_kernel, blk=blk, group=group),
        grid=(nb, H_A // PAIR, l // rows),
        in_specs=[qo, kv, kv],
        out_specs=qo,
        out_shape=jax.ShapeDtypeStruct((nb, l, W_A), BF16),
        scratch_shapes=[pltpu.VMEM((PAIR * group, blk, LANES), BF16),
                        pltpu.VMEM((blk, blk), BF16),
                        pltpu.VMEM((group, blk, LANES), F32),
                        pltpu.VMEM((PAIR * group, blk, 1), F32)],
        compiler_params=pltpu.CompilerParams(
            dimension_semantics=("arbitrary", "arbitrary", "arbitrary"),
            vmem_limit_bytes=VMEM_LIMIT_BYTES),
        name="stick_prompt",
    )(qn, k, v)


def _stick_sample_kernel(q_ref, kn_ref, vn_ref, kc_ref, vc_ref, o_ref, tri_ref, acc_ref, r_ref,
                         *, blk, heads):
    lq = q_ref.shape[1]
    n_cache = kc_ref.shape[3] // blk
    tri_ref[...] = _upper_ones(blk)
    tri_new = _upper_ones(lq)

    def q_of(hh):
        return q_ref[0, :, hh * DH_A:(hh + 1) * DH_A]

    for hh in range(heads):
        out, total = _stick_block(q_of(hh), kn_ref[0, hh].astype(BF16), vn_ref[0, hh].astype(BF16),
                                  tri_new, None, diagonal=True)
        acc_ref[hh] = out
        r_ref[hh] = total

    def cond(carry):
        kb, alive = carry
        return jnp.logical_and(kb >= 0, alive > 0)

    def body(carry):
        kb, _ = carry
        start = pl.multiple_of(kb * blk, blk)
        for hh in range(heads):
            k = kc_ref[0, 0, hh, pl.ds(start, blk), :].astype(BF16)
            v = vc_ref[0, 0, hh, pl.ds(start, blk), :].astype(BF16)
            r = r_ref[hh]
            out, total = _stick_block(q_of(hh), k, v, tri_ref[...], r, diagonal=False)
            acc_ref[hh] += out
            r_ref[hh] = r + total
        return kb - 1, _any_row_alive(r_ref)

    lax.while_loop(cond, body, (jnp.int32(n_cache - 1), _any_row_alive(r_ref)))
    for hh in range(heads):
        o_ref[0, :, hh * DH_A:(hh + 1) * DH_A] = acc_ref[hh].astype(BF16)


def _stick_sample(qn, k_new, v_new, k_cache, v_cache, depth_idx, blk, heads):
    nb, lq, _ = qn.shape
    past = k_cache.shape[3]
    new = pl.BlockSpec((1, heads, lq, DH_A), lambda b, h: (b, h, 0, 0))
    old = pl.BlockSpec((1, 1, heads, past, DH_A), lambda b, h: (depth_idx, b, h, 0, 0))
    qo = pl.BlockSpec((1, lq, heads * DH_A), lambda b, h: (b, 0, h))
    return pl.pallas_call(
        functools.partial(_stick_sample_kernel, blk=blk, heads=heads),
        grid=(nb, H_A // heads),
        in_specs=[qo, new, new, old, old],
        out_specs=qo,
        out_shape=jax.ShapeDtypeStruct((nb, lq, W_A), BF16),
        scratch_shapes=[pltpu.VMEM((blk, blk), BF16),
                        pltpu.VMEM((heads, lq, DH_A), F32),
                        pltpu.VMEM((heads, lq, 1), F32)],
        compiler_params=pltpu.CompilerParams(
            dimension_semantics=("arbitrary", "arbitrary"), vmem_limit_bytes=VMEM_LIMIT_BYTES),
        name="stick_sample",
    )(qn, k_new, v_new, k_cache, v_cache)


def _retention_kernel(lg_ref, q_ref, k_ref, v_ref, gr_ref, s0_ref, o_ref, s1_ref,
                      state_ref, decay_ref, *, chunk):
    c = pl.program_id(1)

    @pl.when(c == 0)
    def _():
        i = lax.broadcasted_iota(jnp.int32, (chunk, chunk), 0)
        j = lax.broadcasted_iota(jnp.int32, (chunk, chunk), 1)
        diff = (i - j).astype(F32)
        for hh in range(H_B):
            state_ref[hh] = s0_ref[0, hh]
            decay_ref[hh] = jnp.where(diff >= 0, jnp.exp(jnp.maximum(diff, 0.0) * lg_ref[hh]), 0.0)

    idx = lax.broadcasted_iota(jnp.int32, (chunk, 1), 0).astype(F32)
    for hh in range(H_B):
        lg = lg_ref[hh]
        q = q_ref[0, :, hh * DK_B:(hh + 1) * DK_B]
        k = k_ref[0, :, hh * DK_B:(hh + 1) * DK_B]
        v = v_ref[0, :, hh * DV_B:(hh + 1) * DV_B]
        state = state_ref[hh]

        scores = _dot_nt(q, k) * decay_ref[hh]
        o = _dot(scores.astype(BF16), v) + jnp.exp((idx + 1.0) * lg) * _dot(q, state.astype(BF16))
        k_dec = (k.astype(F32) * jnp.exp((chunk - 1.0 - idx) * lg)).T.astype(BF16)
        state_ref[hh] = jnp.exp(chunk * lg) * state + _dot(k_dec, v)

        o = o * lax.rsqrt(jnp.mean(o * o, axis=-1, keepdims=True) + EPS)
        g = gr_ref[0, :, hh * DV_B:(hh + 1) * DV_B].astype(F32)
        o_ref[0, :, hh * DV_B:(hh + 1) * DV_B] = (o * (g * _sigmoid(g))).astype(BF16)

    @pl.when(c == pl.num_programs(1) - 1)
    def _():
        for hh in range(H_B):
            s1_ref[0, hh] = state_ref[hh]


def _retention(log_gamma, q, k, v, gr, s0, s0_lead, chunk):
    nb, l, _ = q.shape
    qk = pl.BlockSpec((1, chunk, QK_B), lambda b, c: (b, c, 0))
    vv = pl.BlockSpec((1, chunk, V_B), lambda b, c: (b, c, 0))
    st = pl.BlockSpec((1, H_B, DK_B, DV_B), lambda b, c: (b, 0, 0, 0))
    nlead = len(s0_lead)
    st_in = pl.BlockSpec((None,) * nlead + (1, H_B, DK_B, DV_B),
                         lambda b, c: (*s0_lead, b, 0, 0, 0))
    return pl.pallas_call(
        functools.partial(_retention_kernel, chunk=chunk),
        grid=(nb, l // chunk),
        in_specs=[pl.BlockSpec(memory_space=pltpu.SMEM), qk, qk, vv, vv, st_in],
        out_specs=[vv, st],
        out_shape=[jax.ShapeDtypeStruct((nb, l, V_B), BF16),
                   jax.ShapeDtypeStruct((nb, H_B, DK_B, DV_B), F32)],
        scratch_shapes=[pltpu.VMEM((H_B, DK_B, DV_B), F32), pltpu.VMEM((H_B, chunk, chunk), F32)],
        compiler_params=pltpu.CompilerParams(
            dimension_semantics=("arbitrary", "arbitrary"), vmem_limit_bytes=VMEM_LIMIT_BYTES),
        name="retention",
    )(log_gamma, q, k, v, gr, s0)


def _out_ffn_kernel(x_ref, oa_ref, ob_ref, ga_ref, gb_ref, cbuf_ref, bg_ref,
                    wpa_ref, wpb_ref, wo_ref, gffn_ref, wa_ref, wb_ref, wc_ref, bc_ref,
                    wd_ref, gfin_ref, y_ref, cnew_ref, carry_ref, *, final):
    tm = x_ref.shape[1]

    @pl.when(pl.program_id(1) == 0)
    def _():
        carry_ref[...] = cbuf_ref[0]

    gate_a = _sigmoid(ga_ref[0].astype(F32) + bg_ref[0:1, :])
    gate_b = _sigmoid(gb_ref[0].astype(F32) + bg_ref[1:2, :])
    mix = gate_a * _dot(oa_ref[0], wpa_ref[...]) + gate_b * _dot(ob_ref[0], wpb_ref[...])
    x1 = x_ref[0] + _dot(mix.astype(BF16), wo_ref[...])

    hn = _rmsnorm(x1, gffn_ref[...]).astype(BF16)
    a = _dot(hn, wa_ref[...])
    up = _dot(hn, wb_ref[...])
    prev2 = carry_ref[0:1, :]
    prev1 = carry_ref[1:2, :]
    row = lax.broadcasted_iota(jnp.int32, (tm, 1), 0)
    a_m1 = jnp.where(row == 0, prev1, pltpu.roll(a, 1, axis=0))
    a_m2 = jnp.where(row == 0, prev2, jnp.where(row == 1, prev1, pltpu.roll(a, 2, axis=0)))
    conv = wc_ref[0:1, :] * a_m2 + wc_ref[1:2, :] * a_m1 + wc_ref[2:3, :] * a + bc_ref[...]
    hid = conv * _sigmoid(conv) * up
    x2 = x1 + _dot(hid.astype(BF16), wd_ref[...])

    y_ref[0] = _rmsnorm(x2, gfin_ref[...]) if final else x2
    tail = a[tm - (CONV_W - 1):, :]
    carry_ref[...] = tail
    cnew_ref[0] = tail


def _out_ffn(x, oa, ob, ga, gb, conv_buf, wts, g_final, tm, final):
    b_gate, w_pa, w_pb, w_o, g_ffn, w_a, w_b, w_conv, b_conv, w_down = wts
    nb, l, _ = x.shape
    tok = lambda width: pl.BlockSpec((1, tm, width), lambda b, i: (b, i, 0))
    cb = pl.BlockSpec((1, CONV_W - 1, D_FF), lambda b, i: (b, 0, 0))
    consts = [b_gate, w_pa, w_pb, w_o, g_ffn, w_a, w_b, w_conv, b_conv, w_down, g_final]
    return pl.pallas_call(
        functools.partial(_out_ffn_kernel, final=final),
        grid=(nb, l // tm),
        in_specs=[tok(D_MODEL), tok(W_A), tok(V_B), tok(D_MODEL), tok(D_MODEL), cb]
                 + [_resident(c.shape) for c in consts],
        out_specs=[tok(D_MODEL), cb],
        out_shape=[jax.ShapeDtypeStruct((nb, l, D_MODEL), F32),
                   jax.ShapeDtypeStruct((nb, CONV_W - 1, D_FF), F32)],
        scratch_shapes=[pltpu.VMEM((CONV_W - 1, D_FF), F32)],
        compiler_params=pltpu.CompilerParams(
            dimension_semantics=("arbitrary", "arbitrary"), vmem_limit_bytes=VMEM_LIMIT_BYTES),
        name="out_ffn",
    )(x, oa, ob, ga, gb, conv_buf, *consts)


def _rope_tables(pos):
    half = DK_B // 2
    inv_freq = ROPE_BASE ** (-jnp.arange(half, dtype=F32) / half)
    ang = pos.astype(F32)[:, None] * inv_freq[None, :]
    cos, sin = jnp.cos(ang), jnp.sin(ang)
    return jnp.concatenate([cos, cos], axis=-1), jnp.concatenate([-sin, sin], axis=-1)


def _layer(x, pos, caches, depth_idx, wts, g_final, final, tiles):
    g_mix, w_in, *rest = wts
    nb = x.shape[0]
    cos, sin = _rope_tables(pos)
    fresh = caches is None
    qa, ka, va, qb, kb, vb, gr, ga, gb, *kv16 = _in_proj(
        x, g_mix, w_in, cos, sin, tiles["in"], token_major_kv=fresh)

    log_gamma = jnp.log1p(-jnp.exp2(-5.0 - jnp.arange(H_B, dtype=F32)))
    if fresh:
        o_a = _stick_prompt(qa, kv16[0], kv16[1], tiles["stick"], tiles["stick_group"])
        ret_state, lead = jnp.zeros((nb, H_B, DK_B, DV_B), F32), ()
        conv_buf = jnp.zeros((nb, CONV_W - 1, D_FF), F32)
    else:
        cache_k, cache_v, ret_state, state_conv = caches
        o_a = _stick_sample(qa, ka, va, cache_k, cache_v, depth_idx, tiles["stick"],
                            tiles["stick_heads"])
        lead = (depth_idx,)
        conv_buf = state_conv[depth_idx]
    o_b, s_new = _retention(log_gamma, qb, kb, vb, gr, ret_state, lead, tiles["ret"])
    y, c_new = _out_ffn(x, o_a, o_b, ga, gb, conv_buf, rest, g_final, tiles["out"], final)
    return y, ka, va, s_new, c_new


PROMPT_TILES = {"in": 512, "stick": 256, "stick_group": 2, "ret": 256, "out": 256}


def kernel(x_prompt, x_sample, cache_k_sb, cache_v_sb, state_ret, state_conv, g_mix, w_in, b_gate,
           w_pa, w_pb, w_o, g_ffn, w_a, w_b, w_conv, b_conv, w_down, g_final):
    depth = w_in.shape[0]
    past = cache_k_sb.shape[3]
    dec = x_sample.shape[1]
    pos_p = jnp.arange(x_prompt.shape[1], dtype=jnp.int32)
    pos_s = past + jnp.arange(dec, dtype=jnp.int32)
    sample_tiles = {"in": dec, "stick": 256, "stick_heads": 4, "ret": dec, "out": dec}
    g_fin = g_final.reshape(1, D_MODEL)
    caches = (cache_k_sb, cache_v_sb, state_ret, state_conv)

    hp, hs = x_prompt, x_sample
    outs = [[] for _ in range(8)]
    for d in range(depth):
        wts = (g_mix[d].reshape(1, D_MODEL), w_in[d].astype(BF16), b_gate[d],
               w_pa[d].astype(BF16), w_pb[d].astype(BF16), w_o[d].astype(BF16),
               g_ffn[d].reshape(1, D_MODEL), w_a[d].astype(BF16), w_b[d].astype(BF16),
               w_conv[d], b_conv[d].reshape(1, D_FF), w_down[d].astype(BF16))
        final = d == depth - 1
        hp, k1, v1, s1, c1 = _layer(hp, pos_p, None, d, wts, g_fin, final, PROMPT_TILES)
        hs, k2, v2, s2, c2 = _layer(hs, pos_s, caches, d, wts, g_fin, final, sample_tiles)
        for lst, val in zip(outs, (k1, v1, s1, c1, k2, v2, s2, c2)):
            lst.append(val)
    stacked = [jnp.stack(o) for o in outs]
    return (hp, hs, *stacked)
```

```python
import functools

import jax
import jax.numpy as jnp
from jax import lax
from jax.experimental import pallas as pl
from jax.experimental.pallas import tpu as pltpu

F32 = jnp.float32
BF16 = jnp.bfloat16

D_MODEL = 1024
H_A, DH_A = 8, 64
W_A = H_A * DH_A
H_B, DK_B, DV_B = 4, 128, 256
QK_B = H_B * DK_B
V_B = H_B * DV_B
D_FF = 2816
CONV_W = 3
ROPE_BASE = 10000.0
EPS = 1e-6
LANES = 128
SUBLANES = 8
PAIR = LANES // DH_A

C_QA, C_KA, C_VA = 0, W_A, 2 * W_A
C_QB = 3 * W_A
C_KB = C_QB + QK_B
C_VB = C_KB + QK_B
C_GR = C_VB + V_B
C_GA = C_GR + V_B
C_GB = C_GA + D_MODEL
IN_COLS = C_GB + D_MODEL

VMEM_LIMIT_BYTES = 56 * 1024 * 1024

LOG2E = 1.4426950408889634
STICK_DEAD_LOG2 = -110.0 * LOG2E
STICK_NO_KEYS = -1e30


def _dot(a, b):
    return jnp.dot(a, b, preferred_element_type=F32)


def _dot_nt(a, b):
    return lax.dot_general(a, b, (((1,), (1,)), ((), ())), preferred_element_type=F32)


def _rmsnorm(x, g):
    return x * lax.rsqrt(jnp.mean(x * x, axis=-1, keepdims=True) + EPS) * g


def _sigmoid(x):
    return 1.0 / (1.0 + jnp.exp(-x))


def _resident(shape):
    nd = len(shape)
    return pl.BlockSpec(shape, lambda *_: (0,) * nd, pipeline_mode=pl.Buffered(1))


def _in_proj_kernel(x_ref, g_ref, w_ref, cos_ref, sin_ref,
                    qa_ref, ka_ref, va_ref, qb_ref, kb_ref, vb_ref, gr_ref, ga_ref, gb_ref,
                    *kv16_refs):
    h = _rmsnorm(x_ref[0], g_ref[...]).astype(BF16)

    def proj(c0, width):
        return _dot(h, w_ref[:, c0:c0 + width])

    qa_ref[0] = (proj(C_QA, W_A) * (-(DH_A ** -0.5) * LOG2E)).astype(BF16)
    ka = proj(C_KA, W_A)
    va = proj(C_VA, W_A)
    for hh in range(H_A):
        ka_ref[0, hh] = ka[:, hh * DH_A:(hh + 1) * DH_A]
        va_ref[0, hh] = va[:, hh * DH_A:(hh + 1) * DH_A]
    if kv16_refs:
        kv16_refs[0][0] = ka.astype(BF16)
        kv16_refs[1][0] = va.astype(BF16)

    cos = cos_ref[...]
    sin = sin_ref[...]
    qb = proj(C_QB, QK_B)
    kb = proj(C_KB, QK_B)
    for hh in range(H_B):
        sl = slice(hh * DK_B, (hh + 1) * DK_B)
        q = qb[:, sl]
        k = kb[:, sl]
        qb_ref[0, :, sl] = (q * cos + pltpu.roll(q, DK_B // 2, axis=1) * sin).astype(BF16)
        kr = (k * cos + pltpu.roll(k, DK_B // 2, axis=1) * sin) * (DK_B ** -0.5)
        kb_ref[0, :, sl] = kr.astype(BF16)

    vb_ref[0] = proj(C_VB, V_B).astype(BF16)
    gr_ref[0] = proj(C_GR, V_B).astype(BF16)
    ga_ref[0] = proj(C_GA, D_MODEL).astype(BF16)
    gb_ref[0] = proj(C_GB, D_MODEL).astype(BF16)


def _in_proj(x, g, w_in, cos, sin, tm, token_major_kv):
    nb, l, _ = x.shape
    nt = l // tm
    tok = lambda width: pl.BlockSpec((1, tm, width), lambda b, i: (b, i, 0))
    heads = pl.BlockSpec((1, H_A, tm, DH_A), lambda b, i: (b, 0, i, 0))
    tab = pl.BlockSpec((tm, DK_B), lambda b, i: (i, 0))
    sds = jax.ShapeDtypeStruct
    out_specs = [tok(W_A), heads, heads, tok(QK_B), tok(QK_B), tok(V_B), tok(V_B),
                 tok(D_MODEL), tok(D_MODEL)]
    out_shape = [sds((nb, l, W_A), BF16),
                 sds((nb, H_A, l, DH_A), F32), sds((nb, H_A, l, DH_A), F32),
                 sds((nb, l, QK_B), BF16), sds((nb, l, QK_B), BF16),
                 sds((nb, l, V_B), BF16), sds((nb, l, V_B), BF16),
                 sds((nb, l, D_MODEL), BF16), sds((nb, l, D_MODEL), BF16)]
    if token_major_kv:
        out_specs += [tok(W_A), tok(W_A)]
        out_shape += [sds((nb, l, W_A), BF16), sds((nb, l, W_A), BF16)]
    return pl.pallas_call(
        _in_proj_kernel,
        grid=(nb, nt),
        in_specs=[tok(D_MODEL), _resident((1, D_MODEL)), _resident((D_MODEL, IN_COLS)), tab, tab],
        out_specs=out_specs,
        out_shape=out_shape,
        compiler_params=pltpu.CompilerParams(
            dimension_semantics=("arbitrary", "arbitrary"), vmem_limit_bytes=VMEM_LIMIT_BYTES),
        name="in_proj",
    )(x, g, w_in, cos, sin)


def _upper_ones(n):
    j = lax.broadcasted_iota(jnp.int32, (n, n), 0)
    s = lax.broadcasted_iota(jnp.int32, (n, n), 1)
    return jnp.where(j > s, 1.0, 0.0).astype(BF16)


def _neg_abs(x):
    sign = jnp.uint32(0x80000000)
    return lax.bitcast_convert_type(lax.bitcast_convert_type(x, jnp.uint32) | sign, F32)


def _stick_trip(qs, ks, vs, tri, carries, masks):
    nzs, lks = [], []
    for q, k, mask in zip(qs, ks, masks):
        nz = _dot_nt(q, k)
        soft = jnp.log(1.0 + jnp.exp2(_neg_abs(nz))) * LOG2E
        lk = jnp.minimum(nz, 0.0) - soft
        if mask is not None:
            lk = jnp.where(mask, lk, 0.0)
        nzs.append(nz)
        lks.append(lk)
    later_all = _dot(jnp.concatenate([lk.astype(BF16) for lk in lks], axis=0), tri)
    results, row0 = [], 0
    for nz, lk, v, carry, mask in zip(nzs, lks, vs, carries, masks):
        later = later_all[row0:row0 + nz.shape[0]]
        row0 += nz.shape[0]
        log_a = (lk - nz) + later
        if carry is not None:
            log_a = log_a + carry
        a = jnp.exp2(log_a)
        if mask is not None:
            a = jnp.where(mask, a, 0.0)
        results.append((_dot(a.astype(BF16), v), later[:, 0:1] + lk[:, 0:1]))
    return results


def _any_row_alive(r_ref):
    m = r_ref[0]
    for c in range(1, r_ref.shape[0]):
        m = jnp.maximum(m, r_ref[c])
    return (jnp.max(m) > STICK_DEAD_LOG2).astype(jnp.int32)


def _stick_prompt_kernel(q_ref, k_ref, v_ref, o_ref, qm_ref, tri_ref, acc_ref, r_ref,
                         *, blk, group):
    step = pl.program_id(2)
    tri_ref[...] = _upper_ones(blk)
    lane = lax.broadcasted_iota(jnp.int32, (blk, LANES), 1)
    first = lane < DH_A
    for g in range(group):
        qp = q_ref[0, g * blk:(g + 1) * blk, :]
        qm_ref[g, 0:blk] = jnp.where(first, qp, jnp.zeros_like(qp))
        qm_ref[g, blk:PAIR * blk] = jnp.where(first, jnp.zeros_like(qp), qp)

    def load_kv(kbs):
        starts = [pl.multiple_of(kb * blk, blk) for kb in kbs]
        return ([k_ref[0, pl.ds(s, blk), :] for s in starts],
                [v_ref[0, pl.ds(s, blk), :] for s in starts])

    def fold_heads(out):
        return jnp.where(first, out[0:blk], out[blk:PAIR * blk])

    chains = range(group)
    t = lax.broadcasted_iota(jnp.int32, (PAIR * blk, blk), 0) & (blk - 1)
    s = lax.broadcasted_iota(jnp.int32, (PAIR * blk, blk), 1)
    ks, vs = load_kv([step * group + g for g in chains])
    res = _stick_trip([qm_ref[g] for g in chains], ks, vs, tri_ref[...],
                      [None] * group, [s < t] * group)
    for g, (out, total) in enumerate(res):
        acc_ref[g] = fold_heads(out)
        r_ref[g] = total

    def cond(carry):
        dist, alive = carry
        return jnp.logical_and(dist <= step * group + (group - 1), alive > 0)

    def body(carry):
        dist, _ = carry
        kbs = [step * group + g - dist for g in chains]
        ks, vs = load_kv([jnp.maximum(kb, 0) for kb in kbs])
        rs = [jnp.where(kb >= 0, r_ref[g], STICK_NO_KEYS) for g, kb in zip(chains, kbs)]
        res = _stick_trip([qm_ref[g] for g in chains], ks, vs, tri_ref[...], rs, [None] * group)
        for g, (out, total) in enumerate(res):
            acc_ref[g] += fold_heads(out)
            r_ref[g] = rs[g] + total
        return dist + 1, _any_row_alive(r_ref)

    lax.while_loop(cond, body, (jnp.int32(1), _any_row_alive(r_ref)))
    for g in chains:
        o_ref[0, g * blk:(g + 1) * blk, :] = acc_ref[g].astype(BF16)


def _stick_prompt(qn, k, v, blk, group):
    nb, l, _ = qn.shape
    rows = blk * group
    kv = pl.BlockSpec((1, l, LANES), lambda b, p, i: (b, 0, p))
    qo = pl.BlockSpec((1, rows, LANES), lambda b, p, i: (b, i, p))
    return pl.pallas_call(
        functools.partial(_stick_prompt_kernel, blk=blk, group=group),
        grid=(nb, H_A // PAIR, l // rows),
        in_specs=[qo, kv, kv],
        out_specs=qo,
        out_shape=jax.ShapeDtypeStruct((nb, l, W_A), BF16),
        scratch_shapes=[pltpu.VMEM((group, PAIR * blk, LANES), BF16),
                        pltpu.VMEM((blk, blk), BF16),
                        pltpu.VMEM((group, blk, LANES), F32),
                        pltpu.VMEM((group, PAIR * blk, 1), F32)],
        compiler_params=pltpu.CompilerParams(
            dimension_semantics=("arbitrary", "arbitrary", "arbitrary"),
            vmem_limit_bytes=VMEM_LIMIT_BYTES),
        name="stick_prompt",
    )(qn, k, v)


def _stick_sample_kernel(q_ref, kn_ref, vn_ref, kc_ref, vc_ref, o_ref, tri_ref, acc_ref, r_ref,
                         *, blk, heads):
    lq = q_ref.shape[1]
    n_cache = kc_ref.shape[3] // blk
    tri_ref[...] = _upper_ones(blk)
    chains = range(heads)

    def qs():
        return [q_ref[0, :, hh * DH_A:(hh + 1) * DH_A] for hh in chains]

    t = lax.broadcasted_iota(jnp.int32, (lq, lq), 0)
    s = lax.broadcasted_iota(jnp.int32, (lq, lq), 1)
    res = _stick_trip(qs(), [kn_ref[0, hh].astype(BF16) for hh in chains],
                      [vn_ref[0, hh].astype(BF16) for hh in chains], _upper_ones(lq),
                      [None] * heads, [s < t] * heads)
    for hh, (out, total) in enumerate(res):
        acc_ref[hh] = out
        r_ref[hh] = total

    def cond(carry):
        kb, alive = carry
        return jnp.logical_and(kb >= 0, alive > 0)

    def body(carry):
        kb, _ = carry
        start = pl.multiple_of(kb * blk, blk)
        ks = [kc_ref[0, 0, hh, pl.ds(start, blk), :].astype(BF16) for hh in chains]
        vs = [vc_ref[0, 0, hh, pl.ds(start, blk), :].astype(BF16) for hh in chains]
        rs = [r_ref[hh] for hh in chains]
        res = _stick_trip(qs(), ks, vs, tri_ref[...], rs, [None] * heads)
        for hh, (out, total) in enumerate(res):
            acc_ref[hh] += out
            r_ref[hh] = rs[hh] + total
        return kb - 1, _any_row_alive(r_ref)

    lax.while_loop(cond, body, (jnp.int32(n_cache - 1), _any_row_alive(r_ref)))
    for hh in chains:
        o_ref[0, :, hh * DH_A:(hh + 1) * DH_A] = acc_ref[hh].astype(BF16)


def _stick_sample(qn, k_new, v_new, k_cache, v_cache, depth_idx, blk, heads):
    nb, lq, _ = qn.shape
    past = k_cache.shape[3]
    new = pl.BlockSpec((1, heads, lq, DH_A), lambda b, h: (b, h, 0, 0))
    old = pl.BlockSpec((1, 1, heads, past, DH_A), lambda b, h: (depth_idx, b, h, 0, 0))
    qo = pl.BlockSpec((1, lq, heads * DH_A), lambda b, h: (b, 0, h))
    return pl.pallas_call(
        functools.partial(_stick_sample_kernel, blk=blk, heads=heads),
        grid=(nb, H_A // heads),
        in_specs=[qo, new, new, old, old],
        out_specs=qo,
        out_shape=jax.ShapeDtypeStruct((nb, lq, W_A), BF16),
        scratch_shapes=[pltpu.VMEM((blk, blk), BF16),
                        pltpu.VMEM((heads, lq, DH_A), F32),
                        pltpu.VMEM((heads, lq, 1), F32)],
        compiler_params=pltpu.CompilerParams(
            dimension_semantics=("arbitrary", "arbitrary"), vmem_limit_bytes=VMEM_LIMIT_BYTES),
        name="stick_sample",
    )(qn, k_new, v_new, k_cache, v_cache)


def _retention_kernel(lg_ref, q_ref, k_ref, v_ref, gr_ref, s0_ref, o_ref, s1_ref,
                      state_ref, decay_ref, *, chunk):
    c = pl.program_id(1)

    @pl.when(c == 0)
    def _():
        i = lax.broadcasted_iota(jnp.int32, (chunk, chunk), 0)
        j = lax.broadcasted_iota(jnp.int32, (chunk, chunk), 1)
        diff = (i - j).astype(F32)
        for hh in range(H_B):
            state_ref[hh] = s0_ref[0, hh]
            decay_ref[hh] = jnp.where(diff >= 0, jnp.exp(jnp.maximum(diff, 0.0) * lg_ref[hh]), 0.0)

    idx = lax.broadcasted_iota(jnp.int32, (chunk, 1), 0).astype(F32)
    for hh in range(H_B):
        lg = lg_ref[hh]
        q = q_ref[0, :, hh * DK_B:(hh + 1) * DK_B]
        k = k_ref[0, :, hh * DK_B:(hh + 1) * DK_B]
        v = v_ref[0, :, hh * DV_B:(hh + 1) * DV_B]
        state = state_ref[hh]

        scores = _dot_nt(q, k) * decay_ref[hh]
        o = _dot(scores.astype(BF16), v) + jnp.exp((idx + 1.0) * lg) * _dot(q, state.astype(BF16))
        k_dec = (k.astype(F32) * jnp.exp((chunk - 1.0 - idx) * lg)).T.astype(BF16)
        state_ref[hh] = jnp.exp(chunk * lg) * state + _dot(k_dec, v)

        o = o * lax.rsqrt(jnp.mean(o * o, axis=-1, keepdims=True) + EPS)
        g = gr_ref[0, :, hh * DV_B:(hh + 1) * DV_B].astype(F32)
        o_ref[0, :, hh * DV_B:(hh + 1) * DV_B] = (o * (g * _sigmoid(g))).astype(BF16)

    @pl.when(c == pl.num_programs(1) - 1)
    def _():
        for hh in range(H_B):
            s1_ref[0, hh] = state_ref[hh]


def _retention(log_gamma, q, k, v, gr, s0, s0_lead, chunk):
    nb, l, _ = q.shape
    qk = pl.BlockSpec((1, chunk, QK_B), lambda b, c: (b, c, 0))
    vv = pl.BlockSpec((1, chunk, V_B), lambda b, c: (b, c, 0))
    st = pl.BlockSpec((1, H_B, DK_B, DV_B), lambda b, c: (b, 0, 0, 0))
    nlead = len(s0_lead)
    st_in = pl.BlockSpec((None,) * nlead + (1, H_B, DK_B, DV_B),
                         lambda b, c: (*s0_lead, b, 0, 0, 0))
    return pl.pallas_call(
        functools.partial(_retention_kernel, chunk=chunk),
        grid=(nb, l // chunk),
        in_specs=[pl.BlockSpec(memory_space=pltpu.SMEM), qk, qk, vv, vv, st_in],
        out_specs=[vv, st],
        out_shape=[jax.ShapeDtypeStruct((nb, l, V_B), BF16),
                   jax.ShapeDtypeStruct((nb, H_B, DK_B, DV_B), F32)],
        scratch_shapes=[pltpu.VMEM((H_B, DK_B, DV_B), F32), pltpu.VMEM((H_B, chunk, chunk), F32)],
        compiler_params=pltpu.CompilerParams(
            dimension_semantics=("arbitrary", "arbitrary"), vmem_limit_bytes=VMEM_LIMIT_BYTES),
        name="retention",
    )(log_gamma, q, k, v, gr, s0)


def _out_ffn_kernel(x_ref, oa_ref, ob_ref, ga_ref, gb_ref, cbuf_ref, bg_ref,
                    wpa_ref, wpb_ref, wo_ref, gffn_ref, wa_ref, wb_ref, wc_ref, bc_ref,
                    wd_ref, gfin_ref, y_ref, cnew_ref, carry_ref, *, final):
    tm = x_ref.shape[1]

    @pl.when(pl.program_id(1) == 0)
    def _():
        carry_ref[...] = cbuf_ref[0]

    gate_a = _sigmoid(ga_ref[0].astype(F32) + bg_ref[0:1, :])
    gate_b = _sigmoid(gb_ref[0].astype(F32) + bg_ref[1:2, :])
    mix = gate_a * _dot(oa_ref[0], wpa_ref[...]) + gate_b * _dot(ob_ref[0], wpb_ref[...])
    x1 = x_ref[0] + _dot(mix.astype(BF16), wo_ref[...])

    hn = _rmsnorm(x1, gffn_ref[...]).astype(BF16)
    a = _dot(hn, wa_ref[...])
    up = _dot(hn, wb_ref[...])
    prev2 = carry_ref[0:1, :]
    prev1 = carry_ref[1:2, :]
    row = lax.broadcasted_iota(jnp.int32, (SUBLANES, 1), 0)
    a_m1 = pltpu.roll(a, 1, axis=0)
    a_m2 = pltpu.roll(a, 2, axis=0)
    top1 = jnp.where(row == 0, prev1, a_m1[:SUBLANES])
    top2 = jnp.where(row == 0, prev2, jnp.where(row == 1, prev1, a_m2[:SUBLANES]))
    a_m1 = jnp.concatenate([top1, a_m1[SUBLANES:]], axis=0)
    a_m2 = jnp.concatenate([top2, a_m2[SUBLANES:]], axis=0)
    conv = wc_ref[0:1, :] * a_m2 + wc_ref[1:2, :] * a_m1 + wc_ref[2:3, :] * a + bc_ref[...]
    hid = conv * _sigmoid(conv) * up
    x2 = x1 + _dot(hid.astype(BF16), wd_ref[...])

    y_ref[0] = _rmsnorm(x2, gfin_ref[...]) if final else x2
    tail = a[tm - (CONV_W - 1):, :]
    carry_ref[...] = tail
    cnew_ref[0] = tail


def _out_ffn(x, oa, ob, ga, gb, conv_buf, wts, g_final, tm, final):
    b_gate, w_pa, w_pb, w_o, g_ffn, w_a, w_b, w_conv, b_conv, w_down = wts
    nb, l, _ = x.shape
    tok = lambda width: pl.BlockSpec((1, tm, width), lambda b, i: (b, i, 0))
    cb = pl.BlockSpec((1, CONV_W - 1, D_FF), lambda b, i: (b, 0, 0))
    consts = [b_gate, w_pa, w_pb, w_o, g_ffn, w_a, w_b, w_conv, b_conv, w_down, g_final]
    return pl.pallas_call(
        functools.partial(_out_ffn_kernel, final=final),
        grid=(nb, l // tm),
        in_specs=[tok(D_MODEL), tok(W_A), tok(V_B), tok(D_MODEL), tok(D_MODEL), cb]
                 + [_resident(c.shape) for c in consts],
        out_specs=[tok(D_MODEL), cb],
        out_shape=[jax.ShapeDtypeStruct((nb, l, D_MODEL), F32),
                   jax.ShapeDtypeStruct((nb, CONV_W - 1, D_FF), F32)],
        scratch_shapes=[pltpu.VMEM((CONV_W - 1, D_FF), F32)],
        compiler_params=pltpu.CompilerParams(
            dimension_semantics=("arbitrary", "arbitrary"), vmem_limit_bytes=VMEM_LIMIT_BYTES),
        name="out_ffn",
    )(x, oa, ob, ga, gb, conv_buf, *consts)


def _rope_tables(pos):
    half = DK_B // 2
    inv_freq = ROPE_BASE ** (-jnp.arange(half, dtype=F32) / half)
    ang = pos.astype(F32)[:, None] * inv_freq[None, :]
    cos, sin = jnp.cos(ang), jnp.sin(ang)
    return jnp.concatenate([cos, cos], axis=-1), jnp.concatenate([-sin, sin], axis=-1)


def _layer(x, pos, caches, depth_idx, wts, g_final, final, tiles):
    g_mix, w_in, *rest = wts
    nb = x.shape[0]
    cos, sin = _rope_tables(pos)
    fresh = caches is None
    qa, ka, va, qb, kb, vb, gr, ga, gb, *kv16 = _in_proj(
        x, g_mix, w_in, cos, sin, tiles["in"], token_major_kv=fresh)

    log_gamma = jnp.log1p(-jnp.exp2(-5.0 - jnp.arange(H_B, dtype=F32)))
    if fresh:
        o_a = _stick_prompt(qa, kv16[0], kv16[1], tiles["stick"], tiles["stick_group"])
        ret_state, lead = jnp.zeros((nb, H_B, DK_B, DV_B), F32), ()
        conv_buf = jnp.zeros((nb, CONV_W - 1, D_FF), F32)
    else:
        cache_k, cache_v, ret_state, state_conv = caches
        o_a = _stick_sample(qa, ka, va, cache_k, cache_v, depth_idx, tiles["stick"],
                            tiles["stick_heads"])
        lead = (depth_idx,)
        conv_buf = state_conv[depth_idx]
    o_b, s_new = _retention(log_gamma, qb, kb, vb, gr, ret_state, lead, tiles["ret"])
    y, c_new = _out_ffn(x, o_a, o_b, ga, gb, conv_buf, rest, g_final, tiles["out"], final)
    return y, ka, va, s_new, c_new


PROMPT_TILES = {"in": 512, "stick": 256, "stick_group": 4, "ret": 256, "out": 256}


def kernel(x_prompt, x_sample, cache_k_sb, cache_v_sb, state_ret, state_conv, g_mix, w_in, b_gate,
           w_pa, w_pb, w_o, g_ffn, w_a, w_b, w_conv, b_conv, w_down, g_final):
    depth = w_in.shape[0]
    past = cache_k_sb.shape[3]
    dec = x_sample.shape[1]
    pos_p = jnp.arange(x_prompt.shape[1], dtype=jnp.int32)
    pos_s = past + jnp.arange(dec, dtype=jnp.int32)
    sample_tiles = {"in": dec, "stick": 256, "stick_heads": 4, "ret": dec, "out": dec}
    g_fin = g_final.reshape(1, D_MODEL)
    caches = (cache_k_sb, cache_v_sb, state_ret, state_conv)

    hp, hs = x_prompt, x_sample
    outs = [[] for _ in range(8)]
    for d in range(depth):
        wts = (g_mix[d].reshape(1, D_MODEL), w_in[d].astype(BF16), b_gate[d],
               w_pa[d].astype(BF16), w_pb[d].astype(BF16), w_o[d].astype(BF16),
               g_ffn[d].reshape(1, D_MODEL), w_a[d].astype(BF16), w_b[d].astype(BF16),
               w_conv[d], b_conv[d].reshape(1, D_FF), w_down[d].astype(BF16))
        final = d == depth - 1
        hp, k1, v1, s1, c1 = _layer(hp, pos_p, None, d, wts, g_fin, final, PROMPT_TILES)
        hs, k2, v2, s2, c2 = _layer(hs, pos_s, caches, d, wts, g_fin, final, sample_tiles)
        for lst, val in zip(outs, (k1, v1, s1, c1, k2, v2, s2, c2)):
            lst.append(val)
    stacked = [o[0][None] if depth == 1 else jnp.stack(o) for o in outs]
    return (hp, hs, *stacked)
```

```python
import functools

import jax
import jax.numpy as jnp
from jax import lax
from jax.experimental import pallas as pl
from jax.experimental.pallas import tpu as pltpu

F32 = jnp.float32
BF16 = jnp.bfloat16

D_MODEL = 1024
H_A, DH_A = 8, 64
W_A = H_A * DH_A
H_B, DK_B, DV_B = 4, 128, 256
QK_B = H_B * DK_B
V_B = H_B * DV_B
D_FF = 2816
CONV_W = 3
ROPE_BASE = 10000.0
EPS = 1e-6
LANES = 128
SUBLANES = 8
PAIR = LANES // DH_A

C_QA, C_KA, C_VA = 0, W_A, 2 * W_A
C_QB = 3 * W_A
C_KB = C_QB + QK_B
C_VB = C_KB + QK_B
C_GR = C_VB + V_B
C_GA = C_GR + V_B
C_GB = C_GA + D_MODEL
IN_COLS = C_GB + D_MODEL

VMEM_LIMIT_BYTES = 56 * 1024 * 1024

LOG2E = 1.4426950408889634
STICK_DEAD_LOG2 = -110.0 * LOG2E
STICK_NO_KEYS = -1e30


def _dot(a, b):
    return jnp.dot(a, b, preferred_element_type=F32)


def _dot_nt(a, b):
    return lax.dot_general(a, b, (((1,), (1,)), ((), ())), preferred_element_type=F32)


def _rmsnorm(x, g):
    return x * lax.rsqrt(jnp.mean(x * x, axis=-1, keepdims=True) + EPS) * g


def _sigmoid(x):
    return 1.0 / (1.0 + jnp.exp(-x))


def _resident(shape):
    nd = len(shape)
    return pl.BlockSpec(shape, lambda *_: (0,) * nd, pipeline_mode=pl.Buffered(1))


def _in_proj_kernel(x_ref, g_ref, w_ref, cos_ref, sin_ref,
                    qa_ref, ka_ref, va_ref, qb_ref, kb_ref, vb_ref, gr_ref, ga_ref, gb_ref,
                    *kv16_refs):
    h = _rmsnorm(x_ref[0], g_ref[...]).astype(BF16)

    def proj(c0, width):
        return _dot(h, w_ref[:, c0:c0 + width])

    qa_ref[0] = (proj(C_QA, W_A) * (-(DH_A ** -0.5) * LOG2E)).astype(BF16)
    ka = proj(C_KA, W_A)
    va = proj(C_VA, W_A)
    for hh in range(H_A):
        ka_ref[0, hh] = ka[:, hh * DH_A:(hh + 1) * DH_A]
        va_ref[0, hh] = va[:, hh * DH_A:(hh + 1) * DH_A]
    if kv16_refs:
        kv16_refs[0][0] = ka.astype(BF16)
        kv16_refs[1][0] = va.astype(BF16)

    cos = cos_ref[...]
    sin = sin_ref[...]
    qb = proj(C_QB, QK_B)
    kb = proj(C_KB, QK_B)
    for hh in range(H_B):
        sl = slice(hh * DK_B, (hh + 1) * DK_B)
        q = qb[:, sl]
        k = kb[:, sl]
        qb_ref[0, :, sl] = (q * cos + pltpu.roll(q, DK_B // 2, axis=1) * sin).astype(BF16)
        kr = (k * cos + pltpu.roll(k, DK_B // 2, axis=1) * sin) * (DK_B ** -0.5)
        kb_ref[0, :, sl] = kr.astype(BF16)

    vb_ref[0] = proj(C_VB, V_B).astype(BF16)
    gr_ref[0] = proj(C_GR, V_B).astype(BF16)
    ga_ref[0] = proj(C_GA, D_MODEL).astype(BF16)
    gb_ref[0] = proj(C_GB, D_MODEL).astype(BF16)


def _in_proj(x, g, w_in, cos, sin, tm, token_major_kv):
    nb, l, _ = x.shape
    nt = l // tm
    tok = lambda width: pl.BlockSpec((1, tm, width), lambda b, i: (b, i, 0))
    heads = pl.BlockSpec((1, H_A, tm, DH_A), lambda b, i: (b, 0, i, 0))
    tab = pl.BlockSpec((tm, DK_B), lambda b, i: (i, 0))
    sds = jax.ShapeDtypeStruct
    out_specs = [tok(W_A), heads, heads, tok(QK_B), tok(QK_B), tok(V_B), tok(V_B),
                 tok(D_MODEL), tok(D_MODEL)]
    out_shape = [sds((nb, l, W_A), BF16),
                 sds((nb, H_A, l, DH_A), F32), sds((nb, H_A, l, DH_A), F32),
                 sds((nb, l, QK_B), BF16), sds((nb, l, QK_B), BF16),
                 sds((nb, l, V_B), BF16), sds((nb, l, V_B), BF16),
                 sds((nb, l, D_MODEL), BF16), sds((nb, l, D_MODEL), BF16)]
    if token_major_kv:
        out_specs += [tok(W_A), tok(W_A)]
        out_shape += [sds((nb, l, W_A), BF16), sds((nb, l, W_A), BF16)]
    return pl.pallas_call(
        _in_proj_kernel,
        grid=(nb, nt),
        in_specs=[tok(D_MODEL), _resident((1, D_MODEL)), _resident((D_MODEL, IN_COLS)), tab, tab],
        out_specs=out_specs,
        out_shape=out_shape,
        compiler_params=pltpu.CompilerParams(
            dimension_semantics=("arbitrary", "arbitrary"), vmem_limit_bytes=VMEM_LIMIT_BYTES),
        name="in_proj",
    )(x, g, w_in, cos, sin)


def _upper_ones(n):
    j = lax.broadcasted_iota(jnp.int32, (n, n), 0)
    s = lax.broadcasted_iota(jnp.int32, (n, n), 1)
    return jnp.where(j > s, 1.0, 0.0).astype(BF16)


def _stick_trip(qs, ks, vs, tri, carries, masks, kv_transposed=False):
    score_dot, value_dot = (_dot, _dot_nt) if kv_transposed else (_dot_nt, _dot)
    nzs, lks = [], []
    for q, k, mask in zip(qs, ks, masks):
        nz = score_dot(q, k)
        soft = jnp.log(1.0 + jnp.exp2(-jnp.abs(nz))) * LOG2E
        lk = jnp.minimum(nz, 0.0) - soft
        if mask is not None:
            lk = jnp.where(mask, lk, 0.0)
        nzs.append(nz)
        lks.append(lk)
    later_all = _dot(jnp.concatenate([lk.astype(BF16) for lk in lks], axis=0), tri)
    results, row0 = [], 0
    for nz, lk, v, carry, mask in zip(nzs, lks, vs, carries, masks):
        later = later_all[row0:row0 + nz.shape[0]]
        row0 += nz.shape[0]
        log_a = (lk - nz) + later
        if carry is not None:
            log_a = log_a + carry
        a = jnp.exp2(log_a)
        if mask is not None:
            a = jnp.where(mask, a, 0.0)
        results.append((value_dot(a.astype(BF16), v), later[:, 0:1] + lk[:, 0:1]))
    return results


def _any_row_alive(r_ref):
    m = r_ref[0]
    for c in range(1, r_ref.shape[0]):
        m = jnp.maximum(m, r_ref[c])
    return (jnp.max(m) > STICK_DEAD_LOG2).astype(jnp.int32)


def _stick_prompt_kernel(q_ref, k_ref, v_ref, o_ref, qm_ref, tri_ref, acc_ref, r_ref,
                         *, blk, group):
    step = pl.program_id(2)
    tri_ref[...] = _upper_ones(blk)
    lane = lax.broadcasted_iota(jnp.int32, (blk, LANES), 1)
    first = lane < DH_A
    for g in range(group):
        qp = q_ref[0, g * blk:(g + 1) * blk, :]
        qm_ref[g, 0:blk] = jnp.where(first, qp, jnp.zeros_like(qp))
        qm_ref[g, blk:PAIR * blk] = jnp.where(first, jnp.zeros_like(qp), qp)

    def load_kv(kbs):
        starts = [pl.multiple_of(kb * blk, blk) for kb in kbs]
        return ([k_ref[0, pl.ds(s, blk), :] for s in starts],
                [v_ref[0, pl.ds(s, blk), :] for s in starts])

    def fold_heads(out):
        return jnp.where(first, out[0:blk], out[blk:PAIR * blk])

    chains = range(group)
    t = lax.broadcasted_iota(jnp.int32, (PAIR * blk, blk), 0) & (blk - 1)
    s = lax.broadcasted_iota(jnp.int32, (PAIR * blk, blk), 1)
    ks, vs = load_kv([step * group + g for g in chains])
    res = _stick_trip([qm_ref[g] for g in chains], ks, vs, tri_ref[...],
                      [None] * group, [s < t] * group)
    for g, (out, total) in enumerate(res):
        acc_ref[g] = fold_heads(out)
        r_ref[g] = total

    def cond(carry):
        dist, alive = carry
        return jnp.logical_and(dist <= step * group + (group - 1), alive > 0)

    def body(carry):
        dist, _ = carry
        kbs = [step * group + g - dist for g in chains]
        ks, vs = load_kv([jnp.maximum(kb, 0) for kb in kbs])
        rs = [jnp.where(kb >= 0, r_ref[g], STICK_NO_KEYS) for g, kb in zip(chains, kbs)]
        res = _stick_trip([qm_ref[g] for g in chains], ks, vs, tri_ref[...], rs, [None] * group)
        for g, (out, total) in enumerate(res):
            acc_ref[g] += fold_heads(out)
            r_ref[g] = rs[g] + total
        return dist + 1, _any_row_alive(r_ref)

    lax.while_loop(cond, body, (jnp.int32(1), _any_row_alive(r_ref)))
    for g in chains:
        o_ref[0, g * blk:(g + 1) * blk, :] = acc_ref[g].astype(BF16)


def _stick_prompt(qn, k, v, blk, group):
    nb, l, _ = qn.shape
    rows = blk * group
    kv = pl.BlockSpec((1, l, LANES), lambda b, p, i: (b, 0, p))
    qo = pl.BlockSpec((1, rows, LANES), lambda b, p, i: (b, i, p))
    return pl.pallas_call(
        functools.partial(_stick_prompt_kernel, blk=blk, group=group),
        grid=(nb, H_A // PAIR, l // rows),
        in_specs=[qo, kv, kv],
        out_specs=qo,
        out_shape=jax.ShapeDtypeStruct((nb, l, W_A), BF16),
        scratch_shapes=[pltpu.VMEM((group, PAIR * blk, LANES), BF16),
                        pltpu.VMEM((blk, blk), BF16),
                        pltpu.VMEM((group, blk, LANES), F32),
                        pltpu.VMEM((group, PAIR * blk, 1), F32)],
        compiler_params=pltpu.CompilerParams(
            dimension_semantics=("arbitrary", "arbitrary", "arbitrary"),
            vmem_limit_bytes=VMEM_LIMIT_BYTES),
        name="stick_prompt",
    )(qn, k, v)


def _stick_sample_kernel(q_ref, kn_ref, vn_ref, kc_ref, vc_ref, o_ref, tri_ref, acc_ref, r_ref,
                         *, blk, heads):
    lq = q_ref.shape[1]
    n_cache = kc_ref.shape[4] // blk
    tri_ref[...] = _upper_ones(blk)
    chains = range(heads)

    def qs():
        return [q_ref[0, :, hh * DH_A:(hh + 1) * DH_A] for hh in chains]

    t = lax.broadcasted_iota(jnp.int32, (lq, lq), 0)
    s = lax.broadcasted_iota(jnp.int32, (lq, lq), 1)
    res = _stick_trip(qs(), [kn_ref[0, hh].astype(BF16) for hh in chains],
                      [vn_ref[0, hh].astype(BF16) for hh in chains], _upper_ones(lq),
                      [None] * heads, [s < t] * heads)
    for hh, (out, total) in enumerate(res):
        acc_ref[hh] = out
        r_ref[hh] = total

    def cond(carry):
        kb, alive = carry
        return jnp.logical_and(kb >= 0, alive > 0)

    def body(carry):
        kb, _ = carry
        start = pl.multiple_of(kb * blk, blk)
        ks = [kc_ref[0, 0, hh, :, pl.ds(start, blk)].astype(BF16) for hh in chains]
        vs = [vc_ref[0, 0, hh, :, pl.ds(start, blk)].astype(BF16) for hh in chains]
        rs = [r_ref[hh] for hh in chains]
        res = _stick_trip(qs(), ks, vs, tri_ref[...], rs, [None] * heads, kv_transposed=True)
        for hh, (out, total) in enumerate(res):
            acc_ref[hh] += out
            r_ref[hh] = rs[hh] + total
        return kb - 1, _any_row_alive(r_ref)

    lax.while_loop(cond, body, (jnp.int32(n_cache - 1), _any_row_alive(r_ref)))
    for hh in chains:
        o_ref[0, :, hh * DH_A:(hh + 1) * DH_A] = acc_ref[hh].astype(BF16)


def _stick_sample(qn, k_new, v_new, k_cache_t, v_cache_t, depth_idx, blk, heads):
    nb, lq, _ = qn.shape
    past = k_cache_t.shape[4]
    new = pl.BlockSpec((1, heads, lq, DH_A), lambda b, h: (b, h, 0, 0))
    old = pl.BlockSpec((1, 1, heads, DH_A, past), lambda b, h: (depth_idx, b, h, 0, 0))
    qo = pl.BlockSpec((1, lq, heads * DH_A), lambda b, h: (b, 0, h))
    return pl.pallas_call(
        functools.partial(_stick_sample_kernel, blk=blk, heads=heads),
        grid=(nb, H_A // heads),
        in_specs=[qo, new, new, old, old],
        out_specs=qo,
        out_shape=jax.ShapeDtypeStruct((nb, lq, W_A), BF16),
        scratch_shapes=[pltpu.VMEM((blk, blk), BF16),
                        pltpu.VMEM((heads, lq, DH_A), F32),
                        pltpu.VMEM((heads, lq, 1), F32)],
        compiler_params=pltpu.CompilerParams(
            dimension_semantics=("arbitrary", "arbitrary"), vmem_limit_bytes=VMEM_LIMIT_BYTES),
        name="stick_sample",
    )(qn, k_new, v_new, k_cache_t, v_cache_t)


def _retention_kernel(lg_ref, q_ref, k_ref, v_ref, gr_ref, s0_ref, o_ref, s1_ref,
                      state_ref, decay_ref, *, chunk):
    c = pl.program_id(1)

    @pl.when(c == 0)
    def _():
        i = lax.broadcasted_iota(jnp.int32, (chunk, chunk), 0)
        j = lax.broadcasted_iota(jnp.int32, (chunk, chunk), 1)
        diff = (i - j).astype(F32)
        for hh in range(H_B):
            state_ref[hh] = s0_ref[0, hh]
            decay_ref[hh] = jnp.where(diff >= 0, jnp.exp(jnp.maximum(diff, 0.0) * lg_ref[hh]), 0.0)

    idx = lax.broadcasted_iota(jnp.int32, (chunk, 1), 0).astype(F32)
    for hh in range(H_B):
        lg = lg_ref[hh]
        q = q_ref[0, :, hh * DK_B:(hh + 1) * DK_B]
        k = k_ref[0, :, hh * DK_B:(hh + 1) * DK_B]
        v = v_ref[0, :, hh * DV_B:(hh + 1) * DV_B]
        state = state_ref[hh]

        scores = _dot_nt(q, k) * decay_ref[hh]
        o = _dot(scores.astype(BF16), v) + jnp.exp((idx + 1.0) * lg) * _dot(q, state.astype(BF16))
        k_dec = (k.astype(F32) * jnp.exp((chunk - 1.0 - idx) * lg)).T.astype(BF16)
        state_ref[hh] = jnp.exp(chunk * lg) * state + _dot(k_dec, v)

        o = o * lax.rsqrt(jnp.mean(o * o, axis=-1, keepdims=True) + EPS)
        g = gr_ref[0, :, hh * DV_B:(hh + 1) * DV_B].astype(F32)
        o_ref[0, :, hh * DV_B:(hh + 1) * DV_B] = (o * (g * _sigmoid(g))).astype(BF16)

    @pl.when(c == pl.num_programs(1) - 1)
    def _():
        for hh in range(H_B):
            s1_ref[0, hh] = state_ref[hh]


def _retention(log_gamma, q, k, v, gr, s0, s0_lead, chunk):
    nb, l, _ = q.shape
    qk = pl.BlockSpec((1, chunk, QK_B), lambda b, c: (b, c, 0))
    vv = pl.BlockSpec((1, chunk, V_B), lambda b, c: (b, c, 0))
    st = pl.BlockSpec((1, H_B, DK_B, DV_B), lambda b, c: (b, 0, 0, 0))
    nlead = len(s0_lead)
    st_in = pl.BlockSpec((None,) * nlead + (1, H_B, DK_B, DV_B),
                         lambda b, c: (*s0_lead, b, 0, 0, 0))
    return pl.pallas_call(
        functools.partial(_retention_kernel, chunk=chunk),
        grid=(nb, l // chunk),
        in_specs=[pl.BlockSpec(memory_space=pltpu.SMEM), qk, qk, vv, vv, st_in],
        out_specs=[vv, st],
        out_shape=[jax.ShapeDtypeStruct((nb, l, V_B), BF16),
                   jax.ShapeDtypeStruct((nb, H_B, DK_B, DV_B), F32)],
        scratch_shapes=[pltpu.VMEM((H_B, DK_B, DV_B), F32), pltpu.VMEM((H_B, chunk, chunk), F32)],
        compiler_params=pltpu.CompilerParams(
            dimension_semantics=("arbitrary", "arbitrary"), vmem_limit_bytes=VMEM_LIMIT_BYTES),
        name="retention",
    )(log_gamma, q, k, v, gr, s0)


def _out_ffn_kernel(x_ref, oa_ref, ob_ref, ga_ref, gb_ref, cbuf_ref, bg_ref,
                    wpa_ref, wpb_ref, wo_ref, gffn_ref, wa_ref, wb_ref, wc_ref, bc_ref,
                    wd_ref, gfin_ref, y_ref, cnew_ref, carry_ref, *, final):
    tm = x_ref.shape[1]

    @pl.when(pl.program_id(1) == 0)
    def _():
        carry_ref[...] = cbuf_ref[0]

    gate_a = _sigmoid(ga_ref[0].astype(F32) + bg_ref[0:1, :])
    gate_b = _sigmoid(gb_ref[0].astype(F32) + bg_ref[1:2, :])
    mix = gate_a * _dot(oa_ref[0], wpa_ref[...]) + gate_b * _dot(ob_ref[0], wpb_ref[...])
    x1 = x_ref[0] + _dot(mix.astype(BF16), wo_ref[...])

    hn = _rmsnorm(x1, gffn_ref[...]).astype(BF16)
    a = _dot(hn, wa_ref[...])
    up = _dot(hn, wb_ref[...])
    prev2 = carry_ref[0:1, :]
    prev1 = carry_ref[1:2, :]
    row = lax.broadcasted_iota(jnp.int32, (SUBLANES, 1), 0)
    a_m1 = pltpu.roll(a, 1, axis=0)
    a_m2 = pltpu.roll(a, 2, axis=0)
    top1 = jnp.where(row == 0, prev1, a_m1[:SUBLANES])
    top2 = jnp.where(row == 0, prev2, jnp.where(row == 1, prev1, a_m2[:SUBLANES]))
    a_m1 = jnp.concatenate([top1, a_m1[SUBLANES:]], axis=0)
    a_m2 = jnp.concatenate([top2, a_m2[SUBLANES:]], axis=0)
    conv = wc_ref[0:1, :] * a_m2 + wc_ref[1:2, :] * a_m1 + wc_ref[2:3, :] * a + bc_ref[...]
    hid = conv * _sigmoid(conv) * up
    x2 = x1 + _dot(hid.astype(BF16), wd_ref[...])

    y_ref[0] = _rmsnorm(x2, gfin_ref[...]) if final else x2
    tail = a[tm - (CONV_W - 1):, :]
    carry_ref[...] = tail
    cnew_ref[0] = tail


def _out_ffn(x, oa, ob, ga, gb, conv_buf, wts, g_final, tm, final):
    b_gate, w_pa, w_pb, w_o, g_ffn, w_a, w_b, w_conv, b_conv, w_down = wts
    nb, l, _ = x.shape
    tok = lambda width: pl.BlockSpec((1, tm, width), lambda b, i: (b, i, 0))
    cb = pl.BlockSpec((1, CONV_W - 1, D_FF), lambda b, i: (b, 0, 0))
    consts = [b_gate, w_pa, w_pb, w_o, g_ffn, w_a, w_b, w_conv, b_conv, w_down, g_final]
    return pl.pallas_call(
        functools.partial(_out_ffn_kernel, final=final),
        grid=(nb, l // tm),
        in_specs=[tok(D_MODEL), tok(W_A), tok(V_B), tok(D_MODEL), tok(D_MODEL), cb]
                 + [_resident(c.shape) for c in consts],
        out_specs=[tok(D_MODEL), cb],
        out_shape=[jax.ShapeDtypeStruct((nb, l, D_MODEL), F32),
                   jax.ShapeDtypeStruct((nb, CONV_W - 1, D_FF), F32)],
        scratch_shapes=[pltpu.VMEM((CONV_W - 1, D_FF), F32)],
        compiler_params=pltpu.CompilerParams(
            dimension_semantics=("arbitrary", "arbitrary"), vmem_limit_bytes=VMEM_LIMIT_BYTES),
        name="out_ffn",
    )(x, oa, ob, ga, gb, conv_buf, *consts)


def _rope_tables(pos):
    half = DK_B // 2
    inv_freq = ROPE_BASE ** (-jnp.arange(half, dtype=F32) / half)
    ang = pos.astype(F32)[:, None] * inv_freq[None, :]
    cos, sin = jnp.cos(ang), jnp.sin(ang)
    return jnp.concatenate([cos, cos], axis=-1), jnp.concatenate([-sin, sin], axis=-1)


def _layer(x, pos, caches, depth_idx, wts, g_final, final, tiles):
    g_mix, w_in, *rest = wts
    nb = x.shape[0]
    cos, sin = _rope_tables(pos)
    fresh = caches is None
    qa, ka, va, qb, kb, vb, gr, ga, gb, *kv16 = _in_proj(
        x, g_mix, w_in, cos, sin, tiles["in"], token_major_kv=fresh)

    log_gamma = jnp.log1p(-jnp.exp2(-5.0 - jnp.arange(H_B, dtype=F32)))
    if fresh:
        o_a = _stick_prompt(qa, kv16[0], kv16[1], tiles["stick"], tiles["stick_group"])
        ret_state, lead = jnp.zeros((nb, H_B, DK_B, DV_B), F32), ()
        conv_buf = jnp.zeros((nb, CONV_W - 1, D_FF), F32)
    else:
        cache_k, cache_v, ret_state, state_conv = caches
        o_a = _stick_sample(qa, ka, va, cache_k, cache_v, depth_idx, tiles["stick"],
                            tiles["stick_heads"])
        lead = (depth_idx,)
        conv_buf = state_conv[depth_idx]
    o_b, s_new = _retention(log_gamma, qb, kb, vb, gr, ret_state, lead, tiles["ret"])
    y, c_new = _out_ffn(x, o_a, o_b, ga, gb, conv_buf, rest, g_final, tiles["out"], final)
    return y, ka, va, s_new, c_new


PROMPT_TILES = {"in": 512, "stick": 256, "stick_group": 4, "ret": 256, "out": 256}


def kernel(x_prompt, x_sample, cache_k_sb, cache_v_sb, state_ret, state_conv, g_mix, w_in, b_gate,
           w_pa, w_pb, w_o, g_ffn, w_a, w_b, w_conv, b_conv, w_down, g_final):
    depth = w_in.shape[0]
    past = cache_k_sb.shape[3]
    dec = x_sample.shape[1]
    pos_p = jnp.arange(x_prompt.shape[1], dtype=jnp.int32)
    pos_s = past + jnp.arange(dec, dtype=jnp.int32)
    sample_tiles = {"in": dec, "stick": 256, "stick_heads": 4, "ret": dec, "out": dec}
    g_fin = g_final.reshape(1, D_MODEL)
    caches = (jnp.swapaxes(cache_k_sb, 3, 4), jnp.swapaxes(cache_v_sb, 3, 4), state_ret,
              state_conv)

    hp, hs = x_prompt, x_sample
    outs = [[] for _ in range(8)]
    for d in range(depth):
        wts = (g_mix[d].reshape(1, D_MODEL), w_in[d].astype(BF16), b_gate[d],
               w_pa[d].astype(BF16), w_pb[d].astype(BF16), w_o[d].astype(BF16),
               g_ffn[d].reshape(1, D_MODEL), w_a[d].astype(BF16), w_b[d].astype(BF16),
               w_conv[d], b_conv[d].reshape(1, D_FF), w_down[d].astype(BF16))
        final = d == depth - 1
        hp, k1, v1, s1, c1 = _layer(hp, pos_p, None, d, wts, g_fin, final, PROMPT_TILES)
        hs, k2, v2, s2, c2 = _layer(hs, pos_s, caches, d, wts, g_fin, final, sample_tiles)
        for lst, val in zip(outs, (k1, v1, s1, c1, k2, v2, s2, c2)):
            lst.append(val)
    stacked = [o[0][None] if depth == 1 else jnp.stack(o) for o in outs]
    return (hp, hs, *stacked)
```

```python
import functools

import jax
import jax.numpy as jnp
from jax import lax
from jax.experimental import pallas as pl
from jax.experimental.pallas import tpu as pltpu

F32 = jnp.float32
BF16 = jnp.bfloat16

D_MODEL = 1024
H_A, DH_A = 8, 64
W_A = H_A * DH_A
H_B, DK_B, DV_B = 4, 128, 256
QK_B = H_B * DK_B
V_B = H_B * DV_B
D_FF = 2816
CONV_W = 3
ROPE_BASE = 10000.0
EPS = 1e-6
LANES = 128
SUBLANES = 8
PAIR = LANES // DH_A

C_QA, C_KA, C_VA = 0, W_A, 2 * W_A
C_QB = 3 * W_A
C_KB = C_QB + QK_B
C_VB = C_KB + QK_B
C_GR = C_VB + V_B
C_GA = C_GR + V_B
C_GB = C_GA + D_MODEL
IN_COLS = C_GB + D_MODEL

VMEM_LIMIT_BYTES = 56 * 1024 * 1024

LOG2E = 1.4426950408889634
STICK_DEAD_LOG2 = -110.0 * LOG2E
STICK_NO_KEYS = -1e30
STICK_MASKED = 1e30


def _dot(a, b):
    return jnp.dot(a, b, preferred_element_type=F32)


def _dot_nt(a, b):
    return lax.dot_general(a, b, (((1,), (1,)), ((), ())), preferred_element_type=F32)


def _rmsnorm(x, g):
    return x * lax.rsqrt(jnp.mean(x * x, axis=-1, keepdims=True) + EPS) * g


def _sigmoid(x):
    return 1.0 / (1.0 + jnp.exp(-x))


def _resident(shape):
    nd = len(shape)
    return pl.BlockSpec(shape, lambda *_: (0,) * nd, pipeline_mode=pl.Buffered(1))


def _in_proj_kernel(x_ref, g_ref, w_ref, cos_ref, sin_ref,
                    qa_ref, ka_ref, va_ref, qb_ref, kb_ref, vb_ref, gr_ref, ga_ref, gb_ref,
                    *kv16_refs):
    h = _rmsnorm(x_ref[0], g_ref[...]).astype(BF16)

    def proj(c0, width):
        return _dot(h, w_ref[:, c0:c0 + width])

    qa_ref[0] = (proj(C_QA, W_A) * (-(DH_A ** -0.5) * LOG2E)).astype(BF16)
    ka = proj(C_KA, W_A)
    va = proj(C_VA, W_A)
    for hh in range(H_A):
        ka_ref[0, hh] = ka[:, hh * DH_A:(hh + 1) * DH_A]
        va_ref[0, hh] = va[:, hh * DH_A:(hh + 1) * DH_A]
    if kv16_refs:
        kv16_refs[0][0] = ka.astype(BF16)
        kv16_refs[1][0] = va.astype(BF16)

    cos = cos_ref[...]
    sin = sin_ref[...]
    qb = proj(C_QB, QK_B)
    kb = proj(C_KB, QK_B)
    for hh in range(H_B):
        sl = slice(hh * DK_B, (hh + 1) * DK_B)
        q = qb[:, sl]
        k = kb[:, sl]
        qb_ref[0, :, sl] = (q * cos + pltpu.roll(q, DK_B // 2, axis=1) * sin).astype(BF16)
        kr = (k * cos + pltpu.roll(k, DK_B // 2, axis=1) * sin) * (DK_B ** -0.5)
        kb_ref[0, :, sl] = kr.astype(BF16)

    vb_ref[0] = proj(C_VB, V_B).astype(BF16)
    gr_ref[0] = proj(C_GR, V_B).astype(BF16)
    ga_ref[0] = proj(C_GA, D_MODEL).astype(BF16)
    gb_ref[0] = proj(C_GB, D_MODEL).astype(BF16)


def _in_proj(x, g, w_in, cos, sin, tm, token_major_kv):
    nb, l, _ = x.shape
    nt = l // tm
    tok = lambda width: pl.BlockSpec((1, tm, width), lambda b, i: (b, i, 0))
    heads = pl.BlockSpec((1, H_A, tm, DH_A), lambda b, i: (b, 0, i, 0))
    tab = pl.BlockSpec((tm, DK_B), lambda b, i: (i, 0))
    sds = jax.ShapeDtypeStruct
    out_specs = [tok(W_A), heads, heads, tok(QK_B), tok(QK_B), tok(V_B), tok(V_B),
                 tok(D_MODEL), tok(D_MODEL)]
    out_shape = [sds((nb, l, W_A), BF16),
                 sds((nb, H_A, l, DH_A), F32), sds((nb, H_A, l, DH_A), F32),
                 sds((nb, l, QK_B), BF16), sds((nb, l, QK_B), BF16),
                 sds((nb, l, V_B), BF16), sds((nb, l, V_B), BF16),
                 sds((nb, l, D_MODEL), BF16), sds((nb, l, D_MODEL), BF16)]
    if token_major_kv:
        out_specs += [tok(W_A), tok(W_A)]
        out_shape += [sds((nb, l, W_A), BF16), sds((nb, l, W_A), BF16)]
    return pl.pallas_call(
        _in_proj_kernel,
        grid=(nb, nt),
        in_specs=[tok(D_MODEL), _resident((1, D_MODEL)), _resident((D_MODEL, IN_COLS)), tab, tab],
        out_specs=out_specs,
        out_shape=out_shape,
        compiler_params=pltpu.CompilerParams(
            dimension_semantics=("arbitrary", "arbitrary"), vmem_limit_bytes=VMEM_LIMIT_BYTES),
        name="in_proj",
    )(x, g, w_in, cos, sin)


def _upper_ones(n):
    j = lax.broadcasted_iota(jnp.int32, (n, n), 0)
    s = lax.broadcasted_iota(jnp.int32, (n, n), 1)
    return jnp.where(j > s, 1.0, 0.0).astype(BF16)


def _stick_trip(qs, ks, vs, tri, carries, masks, kv_transposed=False):
    score_dot, value_dot = (_dot, _dot_nt) if kv_transposed else (_dot_nt, _dot)
    nzs, lks = [], []
    for q, k, mask in zip(qs, ks, masks):
        nz = score_dot(q, k)
        if mask is not None:
            nz = jnp.where(mask, nz, STICK_MASKED)
        soft = jnp.log(1.0 + jnp.exp2(-jnp.abs(nz))) * LOG2E
        lk = jnp.minimum(nz, 0.0) - soft
        nzs.append(nz)
        lks.append(lk)
    later_all = _dot(jnp.concatenate([lk.astype(BF16) for lk in lks], axis=0), tri)
    results, row0 = [], 0
    for nz, lk, v, carry in zip(nzs, lks, vs, carries):
        later = later_all[row0:row0 + nz.shape[0]]
        row0 += nz.shape[0]
        log_a = (lk - nz) + later
        if carry is not None:
            log_a = log_a + carry
        a = jnp.exp2(log_a)
        results.append((value_dot(a.astype(BF16), v), later[:, 0:1] + lk[:, 0:1]))
    return results


def _any_row_alive(r_ref):
    m = r_ref[0]
    for c in range(1, r_ref.shape[0]):
        m = jnp.maximum(m, r_ref[c])
    return (jnp.max(m) > STICK_DEAD_LOG2).astype(jnp.int32)


def _stick_prompt_kernel(q_ref, k_ref, v_ref, o_ref, qm_ref, tri_ref, acc_ref, r_ref,
                         *, rows, win, group):
    step = pl.program_id(2)
    tri_ref[...] = _upper_ones(win)
    lane = lax.broadcasted_iota(jnp.int32, (rows, LANES), 1)
    first = lane < DH_A
    for g in range(group):
        qp = q_ref[0, g * rows:(g + 1) * rows, :]
        qm_ref[g, 0:rows] = jnp.where(first, qp, jnp.zeros_like(qp))
        qm_ref[g, rows:PAIR * rows] = jnp.where(first, jnp.zeros_like(qp), qp)

    def load_kv(starts):
        starts = [pl.multiple_of(s, rows) for s in starts]
        return ([k_ref[0, pl.ds(s, win), :] for s in starts],
                [v_ref[0, pl.ds(s, win), :] for s in starts])

    def fold_heads(out):
        return jnp.where(first, out[0:rows], out[rows:PAIR * rows])

    chains = range(group)
    q0s = [(step * group + g) * rows for g in chains]
    starts = [jnp.maximum(q0 - (win - rows), 0) for q0 in q0s]
    t = lax.broadcasted_iota(jnp.int32, (PAIR * rows, win), 0) & (rows - 1)
    s = lax.broadcasted_iota(jnp.int32, (PAIR * rows, win), 1)
    ks, vs = load_kv(starts)
    masks = [s - t < q0 - start for q0, start in zip(q0s, starts)]
    res = _stick_trip([qm_ref[g] for g in chains], ks, vs, tri_ref[...], [None] * group, masks)
    for g, (out, total) in enumerate(res):
        acc_ref[g] = fold_heads(out)
        r_ref[g] = total

    def cond(carry):
        _, alive = carry
        return alive > 0

    def body(carry):
        trip, _ = carry
        limits = [start - win * (trip - 1) for start in starts]
        firsts = [jnp.maximum(limit - win, 0) for limit in limits]
        ks, vs = load_kv(firsts)
        rs = [jnp.where(limit > 0, r_ref[g], STICK_NO_KEYS) for g, limit in zip(chains, limits)]
        masks = [s < limit - lo for limit, lo in zip(limits, firsts)]
        res = _stick_trip([qm_ref[g] for g in chains], ks, vs, tri_ref[...], rs, masks)
        for g, (out, total) in enumerate(res):
            acc_ref[g] += fold_heads(out)
            r_ref[g] = rs[g] + total
        return trip + 1, _any_row_alive(r_ref)

    lax.while_loop(cond, body, (jnp.int32(1), _any_row_alive(r_ref)))
    for g in chains:
        o_ref[0, g * rows:(g + 1) * rows, :] = acc_ref[g].astype(BF16)


def _stick_prompt(qn, k, v, rows, win, group):
    nb, l, _ = qn.shape
    step_rows = rows * group
    kv = pl.BlockSpec((1, l, LANES), lambda b, p, i: (b, 0, p))
    qo = pl.BlockSpec((1, step_rows, LANES), lambda b, p, i: (b, i, p))
    return pl.pallas_call(
        functools.partial(_stick_prompt_kernel, rows=rows, win=win, group=group),
        grid=(nb, H_A // PAIR, l // step_rows),
        in_specs=[qo, kv, kv],
        out_specs=qo,
        out_shape=jax.ShapeDtypeStruct((nb, l, W_A), BF16),
        scratch_shapes=[pltpu.VMEM((group, PAIR * rows, LANES), BF16),
                        pltpu.VMEM((win, win), BF16),
                        pltpu.VMEM((group, rows, LANES), F32),
                        pltpu.VMEM((group, PAIR * rows, 1), F32)],
        compiler_params=pltpu.CompilerParams(
            dimension_semantics=("arbitrary", "arbitrary", "arbitrary"),
            vmem_limit_bytes=VMEM_LIMIT_BYTES),
        name="stick_prompt",
    )(qn, k, v)


def _stick_sample_kernel(q_ref, kn_ref, vn_ref, kc_ref, vc_ref, o_ref, tri_ref, acc_ref, r_ref,
                         *, blk, heads):
    lq = q_ref.shape[1]
    n_cache = kc_ref.shape[4] // blk
    tri_ref[...] = _upper_ones(blk)
    chains = range(heads)

    def qs():
        return [q_ref[0, :, hh * DH_A:(hh + 1) * DH_A] for hh in chains]

    t = lax.broadcasted_iota(jnp.int32, (lq, lq), 0)
    s = lax.broadcasted_iota(jnp.int32, (lq, lq), 1)
    res = _stick_trip(qs(), [kn_ref[0, hh].astype(BF16) for hh in chains],
                      [vn_ref[0, hh].astype(BF16) for hh in chains], _upper_ones(lq),
                      [None] * heads, [s < t] * heads)
    for hh, (out, total) in enumerate(res):
        acc_ref[hh] = out
        r_ref[hh] = total

    def cond(carry):
        kb, alive = carry
        return jnp.logical_and(kb >= 0, alive > 0)

    def body(carry):
        kb, _ = carry
        start = pl.multiple_of(kb * blk, blk)
        ks = [kc_ref[0, 0, hh, :, pl.ds(start, blk)].astype(BF16) for hh in chains]
        vs = [vc_ref[0, 0, hh, :, pl.ds(start, blk)].astype(BF16) for hh in chains]
        rs = [r_ref[hh] for hh in chains]
        res = _stick_trip(qs(), ks, vs, tri_ref[...], rs, [None] * heads, kv_transposed=True)
        for hh, (out, total) in enumerate(res):
            acc_ref[hh] += out
            r_ref[hh] = rs[hh] + total
        return kb - 1, _any_row_alive(r_ref)

    lax.while_loop(cond, body, (jnp.int32(n_cache - 1), _any_row_alive(r_ref)))
    for hh in chains:
        o_ref[0, :, hh * DH_A:(hh + 1) * DH_A] = acc_ref[hh].astype(BF16)


def _stick_sample(qn, k_new, v_new, k_cache_t, v_cache_t, depth_idx, blk, heads):
    nb, lq, _ = qn.shape
    past = k_cache_t.shape[4]
    new = pl.BlockSpec((1, heads, lq, DH_A), lambda b, h: (b, h, 0, 0))
    old = pl.BlockSpec((1, 1, heads, DH_A, past), lambda b, h: (depth_idx, b, h, 0, 0))
    qo = pl.BlockSpec((1, lq, heads * DH_A), lambda b, h: (b, 0, h))
    return pl.pallas_call(
        functools.partial(_stick_sample_kernel, blk=blk, heads=heads),
        grid=(nb, H_A // heads),
        in_specs=[qo, new, new, old, old],
        out_specs=qo,
        out_shape=jax.ShapeDtypeStruct((nb, lq, W_A), BF16),
        scratch_shapes=[pltpu.VMEM((blk, blk), BF16),
                        pltpu.VMEM((heads, lq, DH_A), F32),
                        pltpu.VMEM((heads, lq, 1), F32)],
        compiler_params=pltpu.CompilerParams(
            dimension_semantics=("arbitrary", "arbitrary"), vmem_limit_bytes=VMEM_LIMIT_BYTES),
        name="stick_sample",
    )(qn, k_new, v_new, k_cache_t, v_cache_t)


def _retention_kernel(lg_ref, q_ref, k_ref, v_ref, gr_ref, s0_ref, o_ref, s1_ref,
                      state_ref, decay_ref, *, chunk):
    c = pl.program_id(1)

    @pl.when(c == 0)
    def _():
        i = lax.broadcasted_iota(jnp.int32, (chunk, chunk), 0)
        j = lax.broadcasted_iota(jnp.int32, (chunk, chunk), 1)
        diff = (i - j).astype(F32)
        for hh in range(H_B):
            state_ref[hh] = s0_ref[0, hh]
            decay_ref[hh] = jnp.where(diff >= 0, jnp.exp(jnp.maximum(diff, 0.0) * lg_ref[hh]), 0.0)

    idx = lax.broadcasted_iota(jnp.int32, (chunk, 1), 0).astype(F32)
    for hh in range(H_B):
        lg = lg_ref[hh]
        q = q_ref[0, :, hh * DK_B:(hh + 1) * DK_B]
        k = k_ref[0, :, hh * DK_B:(hh + 1) * DK_B]
        v = v_ref[0, :, hh * DV_B:(hh + 1) * DV_B]
        state = state_ref[hh]

        scores = _dot_nt(q, k) * decay_ref[hh]
        o = _dot(scores.astype(BF16), v) + jnp.exp((idx + 1.0) * lg) * _dot(q, state.astype(BF16))
        k_dec = (k.astype(F32) * jnp.exp((chunk - 1.0 - idx) * lg)).T.astype(BF16)
        state_ref[hh] = jnp.exp(chunk * lg) * state + _dot(k_dec, v)

        o = o * lax.rsqrt(jnp.mean(o * o, axis=-1, keepdims=True) + EPS)
        g = gr_ref[0, :, hh * DV_B:(hh + 1) * DV_B].astype(F32)
        o_ref[0, :, hh * DV_B:(hh + 1) * DV_B] = (o * (g * _sigmoid(g))).astype(BF16)

    @pl.when(c == pl.num_programs(1) - 1)
    def _():
        for hh in range(H_B):
            s1_ref[0, hh] = state_ref[hh]


def _retention(log_gamma, q, k, v, gr, s0, s0_lead, chunk):
    nb, l, _ = q.shape
    qk = pl.BlockSpec((1, chunk, QK_B), lambda b, c: (b, c, 0))
    vv = pl.BlockSpec((1, chunk, V_B), lambda b, c: (b, c, 0))
    st = pl.BlockSpec((1, H_B, DK_B, DV_B), lambda b, c: (b, 0, 0, 0))
    nlead = len(s0_lead)
    st_in = pl.BlockSpec((None,) * nlead + (1, H_B, DK_B, DV_B),
                         lambda b, c: (*s0_lead, b, 0, 0, 0))
    return pl.pallas_call(
        functools.partial(_retention_kernel, chunk=chunk),
        grid=(nb, l // chunk),
        in_specs=[pl.BlockSpec(memory_space=pltpu.SMEM), qk, qk, vv, vv, st_in],
        out_specs=[vv, st],
        out_shape=[jax.ShapeDtypeStruct((nb, l, V_B), BF16),
                   jax.ShapeDtypeStruct((nb, H_B, DK_B, DV_B), F32)],
        scratch_shapes=[pltpu.VMEM((H_B, DK_B, DV_B), F32), pltpu.VMEM((H_B, chunk, chunk), F32)],
        compiler_params=pltpu.CompilerParams(
            dimension_semantics=("arbitrary", "arbitrary"), vmem_limit_bytes=VMEM_LIMIT_BYTES),
        name="retention",
    )(log_gamma, q, k, v, gr, s0)


def _out_ffn_kernel(x_ref, oa_ref, ob_ref, ga_ref, gb_ref, cbuf_ref, bg_ref,
                    wpa_ref, wpb_ref, wo_ref, gffn_ref, wa_ref, wb_ref, wc_ref, bc_ref,
                    wd_ref, gfin_ref, y_ref, cnew_ref, carry_ref, *, final):
    tm = x_ref.shape[1]

    @pl.when(pl.program_id(1) == 0)
    def _():
        carry_ref[...] = cbuf_ref[0]

    gate_a = _sigmoid(ga_ref[0].astype(F32) + bg_ref[0:1, :])
    gate_b = _sigmoid(gb_ref[0].astype(F32) + bg_ref[1:2, :])
    mix = gate_a * _dot(oa_ref[0], wpa_ref[...]) + gate_b * _dot(ob_ref[0], wpb_ref[...])
    x1 = x_ref[0] + _dot(mix.astype(BF16), wo_ref[...])

    hn = _rmsnorm(x1, gffn_ref[...]).astype(BF16)
    a = _dot(hn, wa_ref[...])
    up = _dot(hn, wb_ref[...])
    prev2 = carry_ref[0:1, :]
    prev1 = carry_ref[1:2, :]
    row = lax.broadcasted_iota(jnp.int32, (SUBLANES, 1), 0)
    a_m1 = pltpu.roll(a, 1, axis=0)
    a_m2 = pltpu.roll(a, 2, axis=0)
    top1 = jnp.where(row == 0, prev1, a_m1[:SUBLANES])
    top2 = jnp.where(row == 0, prev2, jnp.where(row == 1, prev1, a_m2[:SUBLANES]))
    a_m1 = jnp.concatenate([top1, a_m1[SUBLANES:]], axis=0)
    a_m2 = jnp.concatenate([top2, a_m2[SUBLANES:]], axis=0)
    conv = wc_ref[0:1, :] * a_m2 + wc_ref[1:2, :] * a_m1 + wc_ref[2:3, :] * a + bc_ref[...]
    hid = conv * _sigmoid(conv) * up
    x2 = x1 + _dot(hid.astype(BF16), wd_ref[...])

    y_ref[0] = _rmsnorm(x2, gfin_ref[...]) if final else x2
    tail = a[tm - (CONV_W - 1):, :]
    carry_ref[...] = tail
    cnew_ref[0] = tail


def _out_ffn(x, oa, ob, ga, gb, conv_buf, wts, g_final, tm, final):
    b_gate, w_pa, w_pb, w_o, g_ffn, w_a, w_b, w_conv, b_conv, w_down = wts
    nb, l, _ = x.shape
    tok = lambda width: pl.BlockSpec((1, tm, width), lambda b, i: (b, i, 0))
    cb = pl.BlockSpec((1, CONV_W - 1, D_FF), lambda b, i: (b, 0, 0))
    consts = [b_gate, w_pa, w_pb, w_o, g_ffn, w_a, w_b, w_conv, b_conv, w_down, g_final]
    return pl.pallas_call(
        functools.partial(_out_ffn_kernel, final=final),
        grid=(nb, l // tm),
        in_specs=[tok(D_MODEL), tok(W_A), tok(V_B), tok(D_MODEL), tok(D_MODEL), cb]
                 + [_resident(c.shape) for c in consts],
        out_specs=[tok(D_MODEL), cb],
        out_shape=[jax.ShapeDtypeStruct((nb, l, D_MODEL), F32),
                   jax.ShapeDtypeStruct((nb, CONV_W - 1, D_FF), F32)],
        scratch_shapes=[pltpu.VMEM((CONV_W - 1, D_FF), F32)],
        compiler_params=pltpu.CompilerParams(
            dimension_semantics=("arbitrary", "arbitrary"), vmem_limit_bytes=VMEM_LIMIT_BYTES),
        name="out_ffn",
    )(x, oa, ob, ga, gb, conv_buf, *consts)


def _rope_tables(pos):
    half = DK_B // 2
    inv_freq = ROPE_BASE ** (-jnp.arange(half, dtype=F32) / half)
    ang = pos.astype(F32)[:, None] * inv_freq[None, :]
    cos, sin = jnp.cos(ang), jnp.sin(ang)
    return jnp.concatenate([cos, cos], axis=-1), jnp.concatenate([-sin, sin], axis=-1)


def _layer(x, pos, caches, depth_idx, wts, g_final, final, tiles):
    g_mix, w_in, *rest = wts
    nb = x.shape[0]
    cos, sin = _rope_tables(pos)
    fresh = caches is None
    qa, ka, va, qb, kb, vb, gr, ga, gb, *kv16 = _in_proj(
        x, g_mix, w_in, cos, sin, tiles["in"], token_major_kv=fresh)

    log_gamma = jnp.log1p(-jnp.exp2(-5.0 - jnp.arange(H_B, dtype=F32)))
    if fresh:
        o_a = _stick_prompt(qa, kv16[0], kv16[1], tiles["stick_rows"], tiles["stick"],
                            tiles["stick_group"])
        ret_state, lead = jnp.zeros((nb, H_B, DK_B, DV_B), F32), ()
        conv_buf = jnp.zeros((nb, CONV_W - 1, D_FF), F32)
    else:
        cache_k, cache_v, ret_state, state_conv = caches
        o_a = _stick_sample(qa, ka, va, cache_k, cache_v, depth_idx, tiles["stick"],
                            tiles["stick_heads"])
        lead = (depth_idx,)
        conv_buf = state_conv[depth_idx]
    o_b, s_new = _retention(log_gamma, qb, kb, vb, gr, ret_state, lead, tiles["ret"])
    y, c_new = _out_ffn(x, o_a, o_b, ga, gb, conv_buf, rest, g_final, tiles["out"], final)
    return y, ka, va, s_new, c_new


PROMPT_TILES = {"in": 512, "stick": 256, "stick_rows": 64, "stick_group": 16, "ret": 256,
                "out": 256}


def kernel(x_prompt, x_sample, cache_k_sb, cache_v_sb, state_ret, state_conv, g_mix, w_in, b_gate,
           w_pa, w_pb, w_o, g_ffn, w_a, w_b, w_conv, b_conv, w_down, g_final):
    depth = w_in.shape[0]
    past = cache_k_sb.shape[3]
    dec = x_sample.shape[1]
    pos_p = jnp.arange(x_prompt.shape[1], dtype=jnp.int32)
    pos_s = past + jnp.arange(dec, dtype=jnp.int32)
    sample_tiles = {"in": dec, "stick": 256, "stick_heads": 4, "ret": dec, "out": dec}
    g_fin = g_final.reshape(1, D_MODEL)
    caches = (jnp.swapaxes(cache_k_sb, 3, 4), jnp.swapaxes(cache_v_sb, 3, 4), state_ret,
              state_conv)

    hp, hs = x_prompt, x_sample
    outs = [[] for _ in range(8)]
    for d in range(depth):
        wts = (g_mix[d].reshape(1, D_MODEL), w_in[d].astype(BF16), b_gate[d],
               w_pa[d].astype(BF16), w_pb[d].astype(BF16), w_o[d].astype(BF16),
               g_ffn[d].reshape(1, D_MODEL), w_a[d].astype(BF16), w_b[d].astype(BF16),
               w_conv[d], b_conv[d].reshape(1, D_FF), w_down[d].astype(BF16))
        final = d == depth - 1
        hp, k1, v1, s1, c1 = _layer(hp, pos_p, None, d, wts, g_fin, final, PROMPT_TILES)
        hs, k2, v2, s2, c2 = _layer(hs, pos_s, caches, d, wts, g_fin, final, sample_tiles)
        for lst, val in zip(outs, (k1, v1, s1, c1, k2, v2, s2, c2)):
            lst.append(val)
    stacked = [o[0][None] if depth == 1 else jnp.stack(o) for o in outs]
    return (hp, hs, *stacked)
```

```python
import functools

import jax
import jax.numpy as jnp
from jax import lax
from jax.experimental import pallas as pl
from jax.experimental.pallas import tpu as pltpu

F32 = jnp.float32
BF16 = jnp.bfloat16

D_MODEL = 1024
H_A, DH_A = 8, 64
W_A = H_A * DH_A
H_B, DK_B, DV_B = 4, 128, 256
QK_B = H_B * DK_B
V_B = H_B * DV_B
D_FF = 2816
CONV_W = 3
ROPE_BASE = 10000.0
EPS = 1e-6
LANES = 128
SUBLANES = 8
PAIR = LANES // DH_A

C_QA, C_KA, C_VA = 0, W_A, 2 * W_A
C_QB = 3 * W_A
C_KB = C_QB + QK_B
C_VB = C_KB + QK_B
C_GR = C_VB + V_B
C_GA = C_GR + V_B
C_GB = C_GA + D_MODEL
IN_COLS = C_GB + D_MODEL

VMEM_LIMIT_BYTES = 56 * 1024 * 1024

LOG2E = 1.4426950408889634
STICK_DEAD_LOG2 = -110.0 * LOG2E
STICK_NO_KEYS = -1e30
STICK_MASKED = 1e30


def _dot(a, b):
    return jnp.dot(a, b, preferred_element_type=F32)


def _dot_nt(a, b):
    return lax.dot_general(a, b, (((1,), (1,)), ((), ())), preferred_element_type=F32)


def _rmsnorm(x, g):
    return x * lax.rsqrt(jnp.mean(x * x, axis=-1, keepdims=True) + EPS) * g


def _sigmoid(x):
    return 1.0 / (1.0 + jnp.exp(-x))


def _resident(shape):
    nd = len(shape)
    return pl.BlockSpec(shape, lambda *_: (0,) * nd, pipeline_mode=pl.Buffered(1))


def _in_proj_kernel(x_ref, g_ref, w_ref, cos_ref, sin_ref,
                    qa_ref, ka_ref, va_ref, qb_ref, kb_ref, vb_ref, gr_ref, ga_ref, gb_ref,
                    *kv16_refs):
    h = _rmsnorm(x_ref[0], g_ref[...]).astype(BF16)

    def proj(c0, width):
        return _dot(h, w_ref[:, c0:c0 + width])

    qa_ref[0] = (proj(C_QA, W_A) * (-(DH_A ** -0.5) * LOG2E)).astype(BF16)
    ka = proj(C_KA, W_A)
    va = proj(C_VA, W_A)
    for hh in range(H_A):
        ka_ref[0, hh] = ka[:, hh * DH_A:(hh + 1) * DH_A]
        va_ref[0, hh] = va[:, hh * DH_A:(hh + 1) * DH_A]
    if kv16_refs:
        kv16_refs[0][0] = ka.astype(BF16)
        kv16_refs[1][0] = va.astype(BF16)

    cos = cos_ref[...]
    sin = sin_ref[...]
    qb = proj(C_QB, QK_B)
    kb = proj(C_KB, QK_B)
    for hh in range(H_B):
        sl = slice(hh * DK_B, (hh + 1) * DK_B)
        q = qb[:, sl]
        k = kb[:, sl]
        qb_ref[0, :, sl] = (q * cos + pltpu.roll(q, DK_B // 2, axis=1) * sin).astype(BF16)
        kr = (k * cos + pltpu.roll(k, DK_B // 2, axis=1) * sin) * (DK_B ** -0.5)
        kb_ref[0, :, sl] = kr.astype(BF16)

    vb_ref[0] = proj(C_VB, V_B).astype(BF16)
    gr_ref[0] = proj(C_GR, V_B).astype(BF16)
    ga_ref[0] = proj(C_GA, D_MODEL).astype(BF16)
    gb_ref[0] = proj(C_GB, D_MODEL).astype(BF16)


def _in_proj(x, g, w_in, cos, sin, tm, token_major_kv):
    nb, l, _ = x.shape
    assert l % tm == 0, (l, tm)
    nt = l // tm
    tok = lambda width: pl.BlockSpec((1, tm, width), lambda b, i: (b, i, 0))
    heads = pl.BlockSpec((1, H_A, tm, DH_A), lambda b, i: (b, 0, i, 0))
    tab = pl.BlockSpec((tm, DK_B), lambda b, i: (i, 0))
    sds = jax.ShapeDtypeStruct
    out_specs = [tok(W_A), heads, heads, tok(QK_B), tok(QK_B), tok(V_B), tok(V_B),
                 tok(D_MODEL), tok(D_MODEL)]
    out_shape = [sds((nb, l, W_A), BF16),
                 sds((nb, H_A, l, DH_A), F32), sds((nb, H_A, l, DH_A), F32),
                 sds((nb, l, QK_B), BF16), sds((nb, l, QK_B), BF16),
                 sds((nb, l, V_B), BF16), sds((nb, l, V_B), BF16),
                 sds((nb, l, D_MODEL), BF16), sds((nb, l, D_MODEL), BF16)]
    if token_major_kv:
        out_specs += [tok(W_A), tok(W_A)]
        out_shape += [sds((nb, l, W_A), BF16), sds((nb, l, W_A), BF16)]
    return pl.pallas_call(
        _in_proj_kernel,
        grid=(nb, nt),
        in_specs=[tok(D_MODEL), _resident((1, D_MODEL)), _resident((D_MODEL, IN_COLS)), tab, tab],
        out_specs=out_specs,
        out_shape=out_shape,
        compiler_params=pltpu.CompilerParams(
            dimension_semantics=("arbitrary", "arbitrary"), vmem_limit_bytes=VMEM_LIMIT_BYTES),
        name="in_proj",
    )(x, g, w_in, cos, sin)


def _upper_ones(n):
    j = lax.broadcasted_iota(jnp.int32, (n, n), 0)
    s = lax.broadcasted_iota(jnp.int32, (n, n), 1)
    return jnp.where(j > s, 1.0, 0.0).astype(BF16)


def _stick_trip(qs, ks, vs, tri, carries, masks, kv_transposed=False):
    score_dot, value_dot = (_dot, _dot_nt) if kv_transposed else (_dot_nt, _dot)
    nzs, lks = [], []
    for q, k, mask in zip(qs, ks, masks):
        nz = score_dot(q, k)
        if mask is not None:
            nz = jnp.where(mask, nz, STICK_MASKED)
        soft = jnp.log(1.0 + jnp.exp2(-jnp.abs(nz))) * LOG2E
        lk = jnp.minimum(nz, 0.0) - soft
        nzs.append(nz)
        lks.append(lk)
    later_all = _dot(jnp.concatenate([lk.astype(BF16) for lk in lks], axis=0), tri)
    results, row0 = [], 0
    for nz, lk, v, carry in zip(nzs, lks, vs, carries):
        later = later_all[row0:row0 + nz.shape[0]]
        row0 += nz.shape[0]
        log_a = (lk - nz) + later
        if carry is not None:
            log_a = log_a + carry
        a = jnp.exp2(log_a)
        results.append((value_dot(a.astype(BF16), v), later[:, 0:1] + lk[:, 0:1]))
    return results


def _any_row_alive(r_ref):
    m = r_ref[0]
    for c in range(1, r_ref.shape[0]):
        m = jnp.maximum(m, r_ref[c])
    return (jnp.max(m) > STICK_DEAD_LOG2).astype(jnp.int32)


def _stick_prompt_kernel(q_ref, k_ref, v_ref, o_ref, qm_ref, tri_ref, acc_ref, r_ref,
                         *, rows, win, group):
    step = pl.program_id(2)
    tri_ref[...] = _upper_ones(win)
    lane = lax.broadcasted_iota(jnp.int32, (rows, LANES), 1)
    first = lane < DH_A
    for g in range(group):
        qp = q_ref[0, g * rows:(g + 1) * rows, :]
        qm_ref[g, 0:rows] = jnp.where(first, qp, jnp.zeros_like(qp))
        qm_ref[g, rows:PAIR * rows] = jnp.where(first, jnp.zeros_like(qp), qp)

    def load_kv(starts):
        starts = [pl.multiple_of(s, rows) for s in starts]
        return ([k_ref[0, pl.ds(s, win), :] for s in starts],
                [v_ref[0, pl.ds(s, win), :] for s in starts])

    def fold_heads(out):
        return jnp.where(first, out[0:rows], out[rows:PAIR * rows])

    chains = range(group)
    q0s = [(step * group + g) * rows for g in chains]
    starts = [jnp.maximum(q0 - (win - rows), 0) for q0 in q0s]
    t = lax.broadcasted_iota(jnp.int32, (PAIR * rows, win), 0) & (rows - 1)
    s = lax.broadcasted_iota(jnp.int32, (PAIR * rows, win), 1)
    ks, vs = load_kv(starts)
    masks = [s - t < q0 - start for q0, start in zip(q0s, starts)]
    res = _stick_trip([qm_ref[g] for g in chains], ks, vs, tri_ref[...], [None] * group, masks)
    for g, (out, total) in enumerate(res):
        acc_ref[g] = fold_heads(out)
        r_ref[g] = total

    def cond(carry):
        _, alive = carry
        return alive > 0

    def body(carry):
        trip, _ = carry
        limits = [start - win * (trip - 1) for start in starts]
        firsts = [jnp.maximum(limit - win, 0) for limit in limits]
        ks, vs = load_kv(firsts)
        rs = [jnp.where(limit > 0, r_ref[g], STICK_NO_KEYS) for g, limit in zip(chains, limits)]
        masks = [s < limit - lo for limit, lo in zip(limits, firsts)]
        res = _stick_trip([qm_ref[g] for g in chains], ks, vs, tri_ref[...], rs, masks)
        for g, (out, total) in enumerate(res):
            acc_ref[g] += fold_heads(out)
            r_ref[g] = rs[g] + total
        return trip + 1, _any_row_alive(r_ref)

    lax.while_loop(cond, body, (jnp.int32(1), _any_row_alive(r_ref)))
    for g in chains:
        o_ref[0, g * rows:(g + 1) * rows, :] = acc_ref[g].astype(BF16)


def _stick_prompt(qn, k, v, rows, win, group):
    nb, l, _ = qn.shape
    step_rows = rows * group
    assert l % step_rows == 0 and win % rows == 0 and l >= win, (l, rows, win, group)
    kv = pl.BlockSpec((1, l, LANES), lambda b, p, i: (b, 0, p))
    qo = pl.BlockSpec((1, step_rows, LANES), lambda b, p, i: (b, i, p))
    return pl.pallas_call(
        functools.partial(_stick_prompt_kernel, rows=rows, win=win, group=group),
        grid=(nb, H_A // PAIR, l // step_rows),
        in_specs=[qo, kv, kv],
        out_specs=qo,
        out_shape=jax.ShapeDtypeStruct((nb, l, W_A), BF16),
        scratch_shapes=[pltpu.VMEM((group, PAIR * rows, LANES), BF16),
                        pltpu.VMEM((win, win), BF16),
                        pltpu.VMEM((group, rows, LANES), F32),
                        pltpu.VMEM((group, PAIR * rows, 1), F32)],
        compiler_params=pltpu.CompilerParams(
            dimension_semantics=("arbitrary", "arbitrary", "arbitrary"),
            vmem_limit_bytes=VMEM_LIMIT_BYTES),
        name="stick_prompt",
    )(qn, k, v)


def _stick_sample_kernel(q_ref, kn_ref, vn_ref, kc_ref, vc_ref, o_ref, tri_ref, acc_ref, r_ref,
                         *, blk, heads):
    lq = q_ref.shape[1]
    n_cache = kc_ref.shape[4] // blk
    tri_ref[...] = _upper_ones(blk)
    chains = range(heads)

    def qs():
        return [q_ref[0, :, hh * DH_A:(hh + 1) * DH_A] for hh in chains]

    t = lax.broadcasted_iota(jnp.int32, (lq, lq), 0)
    s = lax.broadcasted_iota(jnp.int32, (lq, lq), 1)
    res = _stick_trip(qs(), [kn_ref[0, hh].astype(BF16) for hh in chains],
                      [vn_ref[0, hh].astype(BF16) for hh in chains], _upper_ones(lq),
                      [None] * heads, [s < t] * heads)
    for hh, (out, total) in enumerate(res):
        acc_ref[hh] = out
        r_ref[hh] = total

    def cond(carry):
        kb, alive = carry
        return jnp.logical_and(kb >= 0, alive > 0)

    def body(carry):
        kb, _ = carry
        start = pl.multiple_of(kb * blk, blk)
        ks = [kc_ref[0, 0, hh, :, pl.ds(start, blk)].astype(BF16) for hh in chains]
        vs = [vc_ref[0, 0, hh, :, pl.ds(start, blk)].astype(BF16) for hh in chains]
        rs = [r_ref[hh] for hh in chains]
        res = _stick_trip(qs(), ks, vs, tri_ref[...], rs, [None] * heads, kv_transposed=True)
        for hh, (out, total) in enumerate(res):
            acc_ref[hh] += out
            r_ref[hh] = rs[hh] + total
        return kb - 1, _any_row_alive(r_ref)

    lax.while_loop(cond, body, (jnp.int32(n_cache - 1), _any_row_alive(r_ref)))
    for hh in chains:
        o_ref[0, :, hh * DH_A:(hh + 1) * DH_A] = acc_ref[hh].astype(BF16)


def _stick_sample(qn, k_new, v_new, k_cache_t, v_cache_t, depth_idx, blk, heads):
    nb, lq, _ = qn.shape
    past = k_cache_t.shape[4]
    new = pl.BlockSpec((1, heads, lq, DH_A), lambda b, h: (b, h, 0, 0))
    old = pl.BlockSpec((1, 1, heads, DH_A, past), lambda b, h: (depth_idx, b, h, 0, 0))
    qo = pl.BlockSpec((1, lq, heads * DH_A), lambda b, h: (b, 0, h))
    return pl.pallas_call(
        functools.partial(_stick_sample_kernel, blk=blk, heads=heads),
        grid=(nb, H_A // heads),
        in_specs=[qo, new, new, old, old],
        out_specs=qo,
        out_shape=jax.ShapeDtypeStruct((nb, lq, W_A), BF16),
        scratch_shapes=[pltpu.VMEM((blk, blk), BF16),
                        pltpu.VMEM((heads, lq, DH_A), F32),
                        pltpu.VMEM((heads, lq, 1), F32)],
        compiler_params=pltpu.CompilerParams(
            dimension_semantics=("arbitrary", "arbitrary"), vmem_limit_bytes=VMEM_LIMIT_BYTES),
        name="stick_sample",
    )(qn, k_new, v_new, k_cache_t, v_cache_t)


def _retention_tile(lg_ref, q_ref, k_ref, v_ref, gr_ref, state_ref, decay_ref):
    chunk = q_ref.shape[1]
    idx = lax.broadcasted_iota(jnp.int32, (chunk, 1), 0).astype(F32)
    outs = []
    for hh in range(H_B):
        lg = lg_ref[hh]
        q = q_ref[0, :, hh * DK_B:(hh + 1) * DK_B]
        k = k_ref[0, :, hh * DK_B:(hh + 1) * DK_B]
        v = v_ref[0, :, hh * DV_B:(hh + 1) * DV_B]
        state = state_ref[hh]

        scores = _dot_nt(q, k) * decay_ref[hh]
        o = _dot(scores.astype(BF16), v) + jnp.exp((idx + 1.0) * lg) * _dot(q, state.astype(BF16))
        k_dec = (k.astype(F32) * jnp.exp((chunk - 1.0 - idx) * lg)).T.astype(BF16)
        state_ref[hh] = jnp.exp(chunk * lg) * state + _dot(k_dec, v)

        o = o * lax.rsqrt(jnp.mean(o * o, axis=-1, keepdims=True) + EPS)
        g = gr_ref[0, :, hh * DV_B:(hh + 1) * DV_B].astype(F32)
        outs.append((o * (g * _sigmoid(g))).astype(BF16))
    return jnp.concatenate(outs, axis=-1)


def _mix_ffn_kernel(lg_ref, x_ref, oa_ref, q_ref, k_ref, v_ref, gr_ref,
                    qn_ref, kn_ref, vn_ref, grn_ref, ga_ref, gb_ref,
                    s0_ref, cbuf_ref, bg_ref, wpa_ref, wpb_ref, wo_ref, gffn_ref, wa_ref, wb_ref,
                    wc_ref, bc_ref, wd_ref, gfin_ref, y_ref, s1_ref, cnew_ref,
                    state_ref, decay_ref, carry_ref, ob_ref, *, final):
    tm = x_ref.shape[1]
    c = pl.program_id(1)

    @pl.when(c == 0)
    def _():
        carry_ref[...] = cbuf_ref[0]
        i = lax.broadcasted_iota(jnp.int32, (tm, tm), 0)
        j = lax.broadcasted_iota(jnp.int32, (tm, tm), 1)
        diff = (i - j).astype(F32)
        for hh in range(H_B):
            state_ref[hh] = s0_ref[0, hh]
            decay_ref[hh] = jnp.where(diff >= 0, jnp.exp(jnp.maximum(diff, 0.0) * lg_ref[hh]), 0.0)
        ob_ref[...] = _retention_tile(lg_ref, q_ref, k_ref, v_ref, gr_ref, state_ref, decay_ref)

    @pl.when(c == pl.num_programs(1) - 1)
    def _():
        for hh in range(H_B):
            s1_ref[0, hh] = state_ref[hh]

    gate_a = _sigmoid(ga_ref[0].astype(F32) + bg_ref[0:1, :])
    gate_b = _sigmoid(gb_ref[0].astype(F32) + bg_ref[1:2, :])
    mix = gate_a * _dot(oa_ref[0], wpa_ref[...]) + gate_b * _dot(ob_ref[...], wpb_ref[...])
    x1 = x_ref[0] + _dot(mix.astype(BF16), wo_ref[...])

    hn = _rmsnorm(x1, gffn_ref[...]).astype(BF16)
    a = _dot(hn, wa_ref[...])
    up = _dot(hn, wb_ref[...])
    ob_ref[...] = _retention_tile(lg_ref, qn_ref, kn_ref, vn_ref, grn_ref, state_ref, decay_ref)
    prev2 = carry_ref[0:1, :]
    prev1 = carry_ref[1:2, :]
    row = lax.broadcasted_iota(jnp.int32, (SUBLANES, 1), 0)
    a_m1 = pltpu.roll(a, 1, axis=0)
    a_m2 = pltpu.roll(a, 2, axis=0)
    top1 = jnp.where(row == 0, prev1, a_m1[:SUBLANES])
    top2 = jnp.where(row == 0, prev2, jnp.where(row == 1, prev1, a_m2[:SUBLANES]))
    a_m1 = jnp.concatenate([top1, a_m1[SUBLANES:]], axis=0)
    a_m2 = jnp.concatenate([top2, a_m2[SUBLANES:]], axis=0)
    conv = wc_ref[0:1, :] * a_m2 + wc_ref[1:2, :] * a_m1 + wc_ref[2:3, :] * a + bc_ref[...]
    hid = conv * _sigmoid(conv) * up
    x2 = x1 + _dot(hid.astype(BF16), wd_ref[...])

    y_ref[0] = _rmsnorm(x2, gfin_ref[...]) if final else x2
    tail = a[tm - (CONV_W - 1):, :]
    carry_ref[...] = tail
    cnew_ref[0] = tail


def _mix_ffn(log_gamma, x, oa, qb, kb, vb, gr, ga, gb, s0, s0_lead, conv_buf, wts, g_final, tm,
             final):
    b_gate, w_pa, w_pb, w_o, g_ffn, w_a, w_b, w_conv, b_conv, w_down = wts
    nb, l, _ = x.shape
    assert l % tm == 0 and tm >= CONV_W - 1, (l, tm)
    tok = lambda width: pl.BlockSpec((1, tm, width), lambda b, i: (b, i, 0))
    cb = pl.BlockSpec((1, CONV_W - 1, D_FF), lambda b, i: (b, 0, 0))
    st = pl.BlockSpec((1, H_B, DK_B, DV_B), lambda b, i: (b, 0, 0, 0))
    st_in = pl.BlockSpec((None,) * len(s0_lead) + (1, H_B, DK_B, DV_B),
                         lambda b, i: (*s0_lead, b, 0, 0, 0))
    consts = [b_gate, w_pa, w_pb, w_o, g_ffn, w_a, w_b, w_conv, b_conv, w_down, g_final]
    last = l // tm - 1
    ahead = lambda width: pl.BlockSpec((1, tm, width),
                                       lambda b, i: (b, jnp.minimum(i + 1, last), 0))
    return pl.pallas_call(
        functools.partial(_mix_ffn_kernel, final=final),
        grid=(nb, l // tm),
        in_specs=[pl.BlockSpec(memory_space=pltpu.SMEM), tok(D_MODEL), tok(W_A),
                  tok(QK_B), tok(QK_B), tok(V_B), tok(V_B),
                  ahead(QK_B), ahead(QK_B), ahead(V_B), ahead(V_B),
                  tok(D_MODEL), tok(D_MODEL), st_in, cb]
                 + [_resident(c.shape) for c in consts],
        out_specs=[tok(D_MODEL), st, cb],
        out_shape=[jax.ShapeDtypeStruct((nb, l, D_MODEL), F32),
                   jax.ShapeDtypeStruct((nb, H_B, DK_B, DV_B), F32),
                   jax.ShapeDtypeStruct((nb, CONV_W - 1, D_FF), F32)],
        scratch_shapes=[pltpu.VMEM((H_B, DK_B, DV_B), F32), pltpu.VMEM((H_B, tm, tm), F32),
                        pltpu.VMEM((CONV_W - 1, D_FF), F32), pltpu.VMEM((tm, V_B), BF16)],
        compiler_params=pltpu.CompilerParams(
            dimension_semantics=("arbitrary", "arbitrary"), vmem_limit_bytes=VMEM_LIMIT_BYTES),
        name="mix_ffn",
    )(log_gamma, x, oa, qb, kb, vb, gr, qb, kb, vb, gr, ga, gb, s0, conv_buf, *consts)


def _rope_tables(pos):
    half = DK_B // 2
    inv_freq = ROPE_BASE ** (-jnp.arange(half, dtype=F32) / half)
    ang = pos.astype(F32)[:, None] * inv_freq[None, :]
    cos, sin = jnp.cos(ang), jnp.sin(ang)
    return jnp.concatenate([cos, cos], axis=-1), jnp.concatenate([-sin, sin], axis=-1)


def _layer(x, pos, caches, depth_idx, wts, g_final, final, tiles):
    g_mix, w_in, *rest = wts
    nb = x.shape[0]
    cos, sin = _rope_tables(pos)
    fresh = caches is None
    qa, ka, va, qb, kb, vb, gr, ga, gb, *kv16 = _in_proj(
        x, g_mix, w_in, cos, sin, tiles["in"], token_major_kv=fresh)

    log_gamma = jnp.log1p(-jnp.exp2(-5.0 - jnp.arange(H_B, dtype=F32)))
    if fresh:
        o_a = _stick_prompt(qa, kv16[0], kv16[1], tiles["stick_rows"], tiles["stick"],
                            tiles["stick_group"])
        ret_state, lead = jnp.zeros((nb, H_B, DK_B, DV_B), F32), ()
        conv_buf = jnp.zeros((nb, CONV_W - 1, D_FF), F32)
    else:
        cache_k, cache_v, ret_state, state_conv = caches
        o_a = _stick_sample(qa, ka, va, cache_k, cache_v, depth_idx, tiles["stick"],
                            tiles["stick_heads"])
        lead = (depth_idx,)
        conv_buf = state_conv[depth_idx]
    y, s_new, c_new = _mix_ffn(log_gamma, x, o_a, qb, kb, vb, gr, ga, gb, ret_state, lead,
                               conv_buf, rest, g_final, tiles["out"], final)
    return y, ka, va, s_new, c_new


PROMPT_TILES = {"in": 512, "stick": 256, "stick_rows": 64, "stick_group": 32, "out": 256}


def kernel(x_prompt, x_sample, cache_k_sb, cache_v_sb, state_ret, state_conv, g_mix, w_in, b_gate,
           w_pa, w_pb, w_o, g_ffn, w_a, w_b, w_conv, b_conv, w_down, g_final):
    depth = w_in.shape[0]
    past = cache_k_sb.shape[3]
    dec = x_sample.shape[1]
    pos_p = jnp.arange(x_prompt.shape[1], dtype=jnp.int32)
    pos_s = past + jnp.arange(dec, dtype=jnp.int32)
    sample_tiles = {"in": dec, "stick": 256, "stick_heads": 4, "out": dec}
    g_fin = g_final.reshape(1, D_MODEL)
    caches = (jnp.swapaxes(cache_k_sb, 3, 4), jnp.swapaxes(cache_v_sb, 3, 4), state_ret,
              state_conv)

    hp, hs = x_prompt, x_sample
    outs = [[] for _ in range(8)]
    for d in range(depth):
        wts = (g_mix[d].reshape(1, D_MODEL), w_in[d].astype(BF16), b_gate[d],
               w_pa[d].astype(BF16), w_pb[d].astype(BF16), w_o[d].astype(BF16),
               g_ffn[d].reshape(1, D_MODEL), w_a[d].astype(BF16), w_b[d].astype(BF16),
               w_conv[d], b_conv[d].reshape(1, D_FF), w_down[d].astype(BF16))
        final = d == depth - 1
        hp, k1, v1, s1, c1 = _layer(hp, pos_p, None, d, wts, g_fin, final, PROMPT_TILES)
        hs, k2, v2, s2, c2 = _layer(hs, pos_s, caches, d, wts, g_fin, final, sample_tiles)
        for lst, val in zip(outs, (k1, v1, s1, c1, k2, v2, s2, c2)):
            lst.append(val)
    stacked = [o[0][None] if depth == 1 else jnp.stack(o) for o in outs]
    return (hp, hs, *stacked)
```

```python
import functools

import jax
import jax.numpy as jnp
from jax import lax
from jax.experimental import pallas as pl
from jax.experimental.pallas import tpu as pltpu

F32 = jnp.float32
BF16 = jnp.bfloat16

D_MODEL = 1024
H_A, DH_A = 8, 64
W_A = H_A * DH_A
H_B, DK_B, DV_B = 4, 128, 256
QK_B = H_B * DK_B
V_B = H_B * DV_B
D_FF = 2816
CONV_W = 3
ROPE_BASE = 10000.0
EPS = 1e-6
LANES = 128
SUBLANES = 8
PAIR = LANES // DH_A

C_QA, C_KA, C_VA = 0, W_A, 2 * W_A
C_QB = 3 * W_A
C_KB = C_QB + QK_B
C_VB = C_KB + QK_B
C_GR = C_VB + V_B
C_GA = C_GR + V_B
C_GB = C_GA + D_MODEL
IN_COLS = C_GB + D_MODEL

VMEM_LIMIT_BYTES = 56 * 1024 * 1024

LOG2E = 1.4426950408889634
STICK_DEAD_LOG2 = -110.0 * LOG2E
STICK_NO_KEYS = -1e30
STICK_MASKED = 1e30


def _dot(a, b):
    return jnp.dot(a, b, preferred_element_type=F32)


def _dot_nt(a, b):
    return lax.dot_general(a, b, (((1,), (1,)), ((), ())), preferred_element_type=F32)


def _rmsnorm(x, g):
    return x * lax.rsqrt(jnp.mean(x * x, axis=-1, keepdims=True) + EPS) * g


def _sigmoid(x):
    return 1.0 / (1.0 + jnp.exp(-x))


def _resident(shape):
    nd = len(shape)
    return pl.BlockSpec(shape, lambda *_: (0,) * nd, pipeline_mode=pl.Buffered(1))


def _in_proj_kernel(x_ref, g_ref, w_ref, cos_ref, sin_ref,
                    qa_ref, ka_ref, va_ref, qb_ref, kb_ref, vb_ref, gr_ref, ga_ref, gb_ref,
                    *kv16_refs):
    h = _rmsnorm(x_ref[0], g_ref[...]).astype(BF16)

    def proj(c0, width):
        return _dot(h, w_ref[:, c0:c0 + width])

    qa_ref[0] = (proj(C_QA, W_A) * (-(DH_A ** -0.5) * LOG2E)).astype(BF16)
    ka = proj(C_KA, W_A)
    va = proj(C_VA, W_A)
    for hh in range(H_A):
        ka_ref[0, hh] = ka[:, hh * DH_A:(hh + 1) * DH_A]
        va_ref[0, hh] = va[:, hh * DH_A:(hh + 1) * DH_A]
    if kv16_refs:
        kv16_refs[0][0] = ka.astype(BF16)
        kv16_refs[1][0] = va.astype(BF16)

    cos = cos_ref[...]
    sin = sin_ref[...]
    qb = proj(C_QB, QK_B)
    kb = proj(C_KB, QK_B)
    for hh in range(H_B):
        sl = slice(hh * DK_B, (hh + 1) * DK_B)
        q = qb[:, sl]
        k = kb[:, sl]
        qb_ref[0, :, sl] = (q * cos + pltpu.roll(q, DK_B // 2, axis=1) * sin).astype(BF16)
        kr = (k * cos + pltpu.roll(k, DK_B // 2, axis=1) * sin) * (DK_B ** -0.5)
        kb_ref[0, :, sl] = kr.astype(BF16)

    vb_ref[0] = proj(C_VB, V_B).astype(BF16)
    gr_ref[0] = proj(C_GR, V_B).astype(BF16)
    ga_ref[0] = proj(C_GA, D_MODEL).astype(BF16)
    gb_ref[0] = proj(C_GB, D_MODEL).astype(BF16)


def _in_proj(x, g, w_in, cos, sin, tm, token_major_kv):
    nb, l, _ = x.shape
    assert l % tm == 0, (l, tm)
    nt = l // tm
    tok = lambda width: pl.BlockSpec((1, tm, width), lambda b, i: (b, i, 0))
    heads = pl.BlockSpec((1, H_A, tm, DH_A), lambda b, i: (b, 0, i, 0))
    tab = pl.BlockSpec((tm, DK_B), lambda b, i: (i, 0))
    sds = jax.ShapeDtypeStruct
    out_specs = [tok(W_A), heads, heads, tok(QK_B), tok(QK_B), tok(V_B), tok(V_B),
                 tok(D_MODEL), tok(D_MODEL)]
    out_shape = [sds((nb, l, W_A), BF16),
                 sds((nb, H_A, l, DH_A), F32), sds((nb, H_A, l, DH_A), F32),
                 sds((nb, l, QK_B), BF16), sds((nb, l, QK_B), BF16),
                 sds((nb, l, V_B), BF16), sds((nb, l, V_B), BF16),
                 sds((nb, l, D_MODEL), BF16), sds((nb, l, D_MODEL), BF16)]
    if token_major_kv:
        out_specs += [tok(W_A), tok(W_A)]
        out_shape += [sds((nb, l, W_A), BF16), sds((nb, l, W_A), BF16)]
    return pl.pallas_call(
        _in_proj_kernel,
        grid=(nb, nt),
        in_specs=[tok(D_MODEL), _resident((1, D_MODEL)), _resident((D_MODEL, IN_COLS)), tab, tab],
        out_specs=out_specs,
        out_shape=out_shape,
        compiler_params=pltpu.CompilerParams(
            dimension_semantics=("arbitrary", "arbitrary"), vmem_limit_bytes=VMEM_LIMIT_BYTES),
        name="in_proj",
    )(x, g, w_in, cos, sin)


def _upper_ones(n):
    j = lax.broadcasted_iota(jnp.int32, (n, n), 0)
    s = lax.broadcasted_iota(jnp.int32, (n, n), 1)
    return jnp.where(j > s, 1.0, 0.0).astype(BF16)


def _stick_trip(qs, ks, vs, tri, carries, masks, kv_transposed=False):
    score_dot, value_dot = (_dot, _dot_nt) if kv_transposed else (_dot_nt, _dot)
    nzs, lks = [], []
    for q, k, mask in zip(qs, ks, masks):
        nz = score_dot(q, k)
        if mask is not None:
            nz = jnp.where(mask, nz, STICK_MASKED)
        soft = jnp.log(1.0 + jnp.exp2(-jnp.abs(nz))) * LOG2E
        lk = jnp.minimum(nz, 0.0) - soft
        nzs.append(nz)
        lks.append(lk)
    later_all = _dot(jnp.concatenate([lk.astype(BF16) for lk in lks], axis=0), tri)
    results, row0 = [], 0
    for nz, lk, v, carry in zip(nzs, lks, vs, carries):
        later = later_all[row0:row0 + nz.shape[0]]
        row0 += nz.shape[0]
        log_a = (lk - nz) + later
        if carry is not None:
            log_a = log_a + carry
        a = jnp.exp2(log_a)
        results.append((value_dot(a.astype(BF16), v), later[:, 0:1] + lk[:, 0:1]))
    return results


def _any_row_alive(r_ref, first_chain=0):
    m = r_ref[first_chain]
    for c in range(first_chain + 1, r_ref.shape[0]):
        m = jnp.maximum(m, r_ref[c])
    return (jnp.max(m) > STICK_DEAD_LOG2).astype(jnp.int32)


def _stick_prompt_kernel(q_ref, k_ref, v_ref, o_ref, qm_ref, tri_ref, acc_ref, r_ref,
                         *, rows, win, group):
    step = pl.program_id(2)
    tri_ref[...] = _upper_ones(win)
    lane = lax.broadcasted_iota(jnp.int32, (rows, LANES), 1)
    first = lane < DH_A
    for g in range(group):
        qp = q_ref[0, g * rows:(g + 1) * rows, :]
        qm_ref[g, 0:rows] = jnp.where(first, qp, jnp.zeros_like(qp))
        qm_ref[g, rows:PAIR * rows] = jnp.where(first, jnp.zeros_like(qp), qp)

    def load_kv(starts):
        starts = [pl.multiple_of(s, rows) for s in starts]
        return ([k_ref[0, pl.ds(s, win), :] for s in starts],
                [v_ref[0, pl.ds(s, win), :] for s in starts])

    def fold_heads(out):
        return jnp.where(first, out[0:rows], out[rows:PAIR * rows])

    chains = range(group)
    q0s = [(step * group + g) * rows for g in chains]
    starts = [jnp.maximum(q0 - (win - rows), 0) for q0 in q0s]
    t = lax.broadcasted_iota(jnp.int32, (PAIR * rows, win), 0) & (rows - 1)
    s = lax.broadcasted_iota(jnp.int32, (PAIR * rows, win), 1)
    ks, vs = load_kv(starts)
    masks = [s - t < q0 - start for q0, start in zip(q0s, starts)]
    res = _stick_trip([qm_ref[g] for g in chains], ks, vs, tri_ref[...], [None] * group, masks)
    for g, (out, total) in enumerate(res):
        acc_ref[g] = fold_heads(out)
        r_ref[g] = total

    def cond(carry):
        _, alive = carry
        return alive > 0

    def body(carry):
        trip, _ = carry
        limits = [start - win * (trip - 1) for start in starts]
        firsts = [jnp.maximum(limit - win, 0) for limit in limits]
        ks, vs = load_kv(firsts)
        rs = [jnp.where(limit > 0, r_ref[g], STICK_NO_KEYS) for g, limit in zip(chains, limits)]
        masks = [s < limit - lo for limit, lo in zip(limits, firsts)]
        res = _stick_trip([qm_ref[g] for g in chains], ks, vs, tri_ref[...], rs, masks)
        for g, (out, total) in enumerate(res):
            acc_ref[g] += fold_heads(out)
            r_ref[g] = rs[g] + total
        return trip + 1, _any_row_alive(r_ref)

    n_done = min(group, (win - rows) // rows + 1)
    if n_done < group:
        alive = jnp.where(step == 0, _any_row_alive(r_ref, n_done), _any_row_alive(r_ref))
    else:
        alive = jnp.where(step == 0, 0, _any_row_alive(r_ref))
    lax.while_loop(cond, body, (jnp.int32(1), alive))
    for g in chains:
        o_ref[0, g * rows:(g + 1) * rows, :] = acc_ref[g].astype(BF16)


def _stick_prompt(qn, k, v, rows, win, group):
    nb, l, _ = qn.shape
    step_rows = rows * group
    assert l % step_rows == 0 and win % rows == 0 and l >= win, (l, rows, win, group)
    kv = pl.BlockSpec((1, l, LANES), lambda b, p, i: (b, 0, p))
    qo = pl.BlockSpec((1, step_rows, LANES), lambda b, p, i: (b, i, p))
    return pl.pallas_call(
        functools.partial(_stick_prompt_kernel, rows=rows, win=win, group=group),
        grid=(nb, H_A // PAIR, l // step_rows),
        in_specs=[qo, kv, kv],
        out_specs=qo,
        out_shape=jax.ShapeDtypeStruct((nb, l, W_A), BF16),
        scratch_shapes=[pltpu.VMEM((group, PAIR * rows, LANES), BF16),
                        pltpu.VMEM((win, win), BF16),
                        pltpu.VMEM((group, rows, LANES), F32),
                        pltpu.VMEM((group, PAIR * rows, 1), F32)],
        compiler_params=pltpu.CompilerParams(
            dimension_semantics=("arbitrary", "arbitrary", "arbitrary"),
            vmem_limit_bytes=VMEM_LIMIT_BYTES),
        name="stick_prompt",
    )(qn, k, v)


def _stick_sample_kernel(q_ref, kn_ref, vn_ref, kc_ref, vc_ref, o_ref, tri_ref, acc_ref, r_ref,
                         *, blk, heads):
    lq = q_ref.shape[1]
    n_cache = kc_ref.shape[4] // blk
    tri_ref[...] = _upper_ones(blk)
    chains = range(heads)

    def qs():
        return [q_ref[0, :, hh * DH_A:(hh + 1) * DH_A] for hh in chains]

    t = lax.broadcasted_iota(jnp.int32, (lq, lq), 0)
    s = lax.broadcasted_iota(jnp.int32, (lq, lq), 1)
    res = _stick_trip(qs(), [kn_ref[0, hh].astype(BF16) for hh in chains],
                      [vn_ref[0, hh].astype(BF16) for hh in chains], _upper_ones(lq),
                      [None] * heads, [s < t] * heads)
    for hh, (out, total) in enumerate(res):
        acc_ref[hh] = out
        r_ref[hh] = total

    def cond(carry):
        kb, alive = carry
        return jnp.logical_and(kb >= 0, alive > 0)

    def body(carry):
        kb, _ = carry
        start = pl.multiple_of(kb * blk, blk)
        ks = [kc_ref[0, 0, hh, :, pl.ds(start, blk)].astype(BF16) for hh in chains]
        vs = [vc_ref[0, 0, hh, :, pl.ds(start, blk)].astype(BF16) for hh in chains]
        rs = [r_ref[hh] for hh in chains]
        res = _stick_trip(qs(), ks, vs, tri_ref[...], rs, [None] * heads, kv_transposed=True)
        for hh, (out, total) in enumerate(res):
            acc_ref[hh] += out
            r_ref[hh] = rs[hh] + total
        return kb - 1, _any_row_alive(r_ref)

    lax.while_loop(cond, body, (jnp.int32(n_cache - 1), _any_row_alive(r_ref)))
    for hh in chains:
        o_ref[0, :, hh * DH_A:(hh + 1) * DH_A] = acc_ref[hh].astype(BF16)


def _stick_sample(qn, k_new, v_new, k_cache_t, v_cache_t, depth_idx, blk, heads):
    nb, lq, _ = qn.shape
    past = k_cache_t.shape[4]
    new = pl.BlockSpec((1, heads, lq, DH_A), lambda b, h: (b, h, 0, 0))
    old = pl.BlockSpec((1, 1, heads, DH_A, past), lambda b, h: (depth_idx, b, h, 0, 0))
    qo = pl.BlockSpec((1, lq, heads * DH_A), lambda b, h: (b, 0, h))
    return pl.pallas_call(
        functools.partial(_stick_sample_kernel, blk=blk, heads=heads),
        grid=(nb, H_A // heads),
        in_specs=[qo, new, new, old, old],
        out_specs=qo,
        out_shape=jax.ShapeDtypeStruct((nb, lq, W_A), BF16),
        scratch_shapes=[pltpu.VMEM((blk, blk), BF16),
                        pltpu.VMEM((heads, lq, DH_A), F32),
                        pltpu.VMEM((heads, lq, 1), F32)],
        compiler_params=pltpu.CompilerParams(
            dimension_semantics=("arbitrary", "arbitrary"), vmem_limit_bytes=VMEM_LIMIT_BYTES),
        name="stick_sample",
    )(qn, k_new, v_new, k_cache_t, v_cache_t)


def _retention_tile(lg_ref, q_ref, k_ref, v_ref, gr_ref, state_ref, decay_ref):
    chunk = q_ref.shape[1]
    idx = lax.broadcasted_iota(jnp.int32, (chunk, 1), 0).astype(F32)
    outs = []
    for hh in range(H_B):
        lg = lg_ref[hh]
        q = q_ref[0, :, hh * DK_B:(hh + 1) * DK_B]
        k = k_ref[0, :, hh * DK_B:(hh + 1) * DK_B]
        v = v_ref[0, :, hh * DV_B:(hh + 1) * DV_B]
        state = state_ref[hh]

        scores = _dot_nt(q, k) * decay_ref[hh]
        o = _dot(scores.astype(BF16), v) + jnp.exp((idx + 1.0) * lg) * _dot(q, state.astype(BF16))
        k_dec = (k.astype(F32) * jnp.exp((chunk - 1.0 - idx) * lg)).T.astype(BF16)
        state_ref[hh] = jnp.exp(chunk * lg) * state + _dot(k_dec, v)

        o = o * lax.rsqrt(jnp.mean(o * o, axis=-1, keepdims=True) + EPS)
        g = gr_ref[0, :, hh * DV_B:(hh + 1) * DV_B].astype(F32)
        outs.append((o * (g * _sigmoid(g))).astype(BF16))
    return jnp.concatenate(outs, axis=-1)


def _mix_ffn_kernel(lg_ref, x_ref, oa_ref, q_ref, k_ref, v_ref, gr_ref,
                    qn_ref, kn_ref, vn_ref, grn_ref, ga_ref, gb_ref,
                    s0_ref, cbuf_ref, bg_ref, wpa_ref, wpb_ref, wo_ref, gffn_ref, wa_ref, wb_ref,
                    wc_ref, bc_ref, wd_ref, gfin_ref, y_ref, s1_ref, cnew_ref,
                    state_ref, decay_ref, carry_ref, ob_ref, *, final):
    tm = x_ref.shape[1]
    c = pl.program_id(1)

    @pl.when(c == 0)
    def _():
        carry_ref[...] = cbuf_ref[0]
        i = lax.broadcasted_iota(jnp.int32, (tm, tm), 0)
        j = lax.broadcasted_iota(jnp.int32, (tm, tm), 1)
        diff = (i - j).astype(F32)
        for hh in range(H_B):
            state_ref[hh] = s0_ref[0, hh]
            decay_ref[hh] = jnp.where(diff >= 0, jnp.exp(jnp.maximum(diff, 0.0) * lg_ref[hh]), 0.0)
        ob_ref[...] = _retention_tile(lg_ref, q_ref, k_ref, v_ref, gr_ref, state_ref, decay_ref)

    @pl.when(c == pl.num_programs(1) - 1)
    def _():
        for hh in range(H_B):
            s1_ref[0, hh] = state_ref[hh]

    gate_a = _sigmoid(ga_ref[0].astype(F32) + bg_ref[0:1, :])
    gate_b = _sigmoid(gb_ref[0].astype(F32) + bg_ref[1:2, :])
    mix = gate_a * _dot(oa_ref[0], wpa_ref[...]) + gate_b * _dot(ob_ref[...], wpb_ref[...])
    x1 = x_ref[0] + _dot(mix.astype(BF16), wo_ref[...])

    hn = _rmsnorm(x1, gffn_ref[...]).astype(BF16)
    a = _dot(hn, wa_ref[...])
    up = _dot(hn, wb_ref[...])
    ob_ref[...] = _retention_tile(lg_ref, qn_ref, kn_ref, vn_ref, grn_ref, state_ref, decay_ref)
    prev2 = carry_ref[0:1, :]
    prev1 = carry_ref[1:2, :]
    row = lax.broadcasted_iota(jnp.int32, (SUBLANES, 1), 0)
    a_m1 = pltpu.roll(a, 1, axis=0)
    a_m2 = pltpu.roll(a, 2, axis=0)
    top1 = jnp.where(row == 0, prev1, a_m1[:SUBLANES])
    top2 = jnp.where(row == 0, prev2, jnp.where(row == 1, prev1, a_m2[:SUBLANES]))
    a_m1 = jnp.concatenate([top1, a_m1[SUBLANES:]], axis=0)
    a_m2 = jnp.concatenate([top2, a_m2[SUBLANES:]], axis=0)
    conv = wc_ref[0:1, :] * a_m2 + wc_ref[1:2, :] * a_m1 + wc_ref[2:3, :] * a + bc_ref[...]
    hid = conv * _sigmoid(conv) * up
    x2 = x1 + _dot(hid.astype(BF16), wd_ref[...])

    y_ref[0] = _rmsnorm(x2, gfin_ref[...]) if final else x2
    tail = a[tm - (CONV_W - 1):, :]
    carry_ref[...] = tail
    cnew_ref[0] = tail


def _mix_ffn(log_gamma, x, oa, qb, kb, vb, gr, ga, gb, s0, s0_lead, conv_buf, wts, g_final, tm,
             final):
    b_gate, w_pa, w_pb, w_o, g_ffn, w_a, w_b, w_conv, b_conv, w_down = wts
    nb, l, _ = x.shape
    assert l % tm == 0 and tm >= CONV_W - 1, (l, tm)
    tok = lambda width: pl.BlockSpec((1, tm, width), lambda b, i: (b, i, 0))
    cb = pl.BlockSpec((1, CONV_W - 1, D_FF), lambda b, i: (b, 0, 0))
    st = pl.BlockSpec((1, H_B, DK_B, DV_B), lambda b, i: (b, 0, 0, 0))
    st_in = pl.BlockSpec((None,) * len(s0_lead) + (1, H_B, DK_B, DV_B),
                         lambda b, i: (*s0_lead, b, 0, 0, 0))
    consts = [b_gate, w_pa, w_pb, w_o, g_ffn, w_a, w_b, w_conv, b_conv, w_down, g_final]
    last = l // tm - 1
    ahead = lambda width: pl.BlockSpec((1, tm, width),
                                       lambda b, i: (b, jnp.minimum(i + 1, last), 0))
    return pl.pallas_call(
        functools.partial(_mix_ffn_kernel, final=final),
        grid=(nb, l // tm),
        in_specs=[pl.BlockSpec(memory_space=pltpu.SMEM), tok(D_MODEL), tok(W_A),
                  tok(QK_B), tok(QK_B), tok(V_B), tok(V_B),
                  ahead(QK_B), ahead(QK_B), ahead(V_B), ahead(V_B),
                  tok(D_MODEL), tok(D_MODEL), st_in, cb]
                 + [_resident(c.shape) for c in consts],
        out_specs=[tok(D_MODEL), st, cb],
        out_shape=[jax.ShapeDtypeStruct((nb, l, D_MODEL), F32),
                   jax.ShapeDtypeStruct((nb, H_B, DK_B, DV_B), F32),
                   jax.ShapeDtypeStruct((nb, CONV_W - 1, D_FF), F32)],
        scratch_shapes=[pltpu.VMEM((H_B, DK_B, DV_B), F32), pltpu.VMEM((H_B, tm, tm), F32),
                        pltpu.VMEM((CONV_W - 1, D_FF), F32), pltpu.VMEM((tm, V_B), BF16)],
        compiler_params=pltpu.CompilerParams(
            dimension_semantics=("arbitrary", "arbitrary"), vmem_limit_bytes=VMEM_LIMIT_BYTES),
        name="mix_ffn",
    )(log_gamma, x, oa, qb, kb, vb, gr, qb, kb, vb, gr, ga, gb, s0, conv_buf, *consts)


def _rope_tables(pos):
    half = DK_B // 2
    inv_freq = ROPE_BASE ** (-jnp.arange(half, dtype=F32) / half)
    ang = pos.astype(F32)[:, None] * inv_freq[None, :]
    cos, sin = jnp.cos(ang), jnp.sin(ang)
    return jnp.concatenate([cos, cos], axis=-1), jnp.concatenate([-sin, sin], axis=-1)


def _layer(x, pos, caches, depth_idx, wts, g_final, final, tiles):
    g_mix, w_in, *rest = wts
    nb, l, _ = x.shape
    cos, sin = _rope_tables(pos)
    fresh = caches is None
    if fresh:
        qa, ka, va, qb, kb, vb, gr, ga, gb, *kv16 = _in_proj(
            x, g_mix, w_in, cos, sin, tiles["in"], token_major_kv=True)
    else:
        flat = _in_proj(x.reshape(1, nb * l, D_MODEL), g_mix, w_in, jnp.tile(cos, (nb, 1)),
                        jnp.tile(sin, (nb, 1)), nb * l, token_major_kv=False)
        unflat = lambda t: t.reshape(nb, l, t.shape[-1])
        heads = lambda t: t.reshape(H_A, nb, l, DH_A).transpose(1, 0, 2, 3)
        qa, ka, va, qb, kb, vb, gr, ga, gb = (
            unflat(flat[0]), heads(flat[1]), heads(flat[2]), *map(unflat, flat[3:]))

    log_gamma = jnp.log1p(-jnp.exp2(-5.0 - jnp.arange(H_B, dtype=F32)))
    if fresh:
        o_a = _stick_prompt(qa, kv16[0], kv16[1], tiles["stick_rows"], tiles["stick"],
                            tiles["stick_group"])
        ret_state, lead = jnp.zeros((nb, H_B, DK_B, DV_B), F32), ()
        conv_buf = jnp.zeros((nb, CONV_W - 1, D_FF), F32)
    else:
        cache_k, cache_v, ret_state, state_conv = caches
        o_a = _stick_sample(qa, ka, va, cache_k, cache_v, depth_idx, tiles["stick"],
                            tiles["stick_heads"])
        lead = (depth_idx,)
        conv_buf = state_conv[depth_idx]
    y, s_new, c_new = _mix_ffn(log_gamma, x, o_a, qb, kb, vb, gr, ga, gb, ret_state, lead,
                               conv_buf, rest, g_final, tiles["out"], final)
    return y, ka, va, s_new, c_new


PROMPT_TILES = {"in": 512, "stick": 256, "stick_rows": 64, "stick_group": 32, "out": 256}


def kernel(x_prompt, x_sample, cache_k_sb, cache_v_sb, state_ret, state_conv, g_mix, w_in, b_gate,
           w_pa, w_pb, w_o, g_ffn, w_a, w_b, w_conv, b_conv, w_down, g_final):
    depth = w_in.shape[0]
    past = cache_k_sb.shape[3]
    dec = x_sample.shape[1]
    pos_p = jnp.arange(x_prompt.shape[1], dtype=jnp.int32)
    pos_s = past + jnp.arange(dec, dtype=jnp.int32)
    sample_tiles = {"in": dec, "stick": 256, "stick_heads": 4, "out": dec}
    g_fin = g_final.reshape(1, D_MODEL)
    caches = (jnp.swapaxes(cache_k_sb, 3, 4), jnp.swapaxes(cache_v_sb, 3, 4), state_ret,
              state_conv)

    hp, hs = x_prompt, x_sample
    outs = [[] for _ in range(8)]
    for d in range(depth):
        wts = (g_mix[d].reshape(1, D_MODEL), w_in[d].astype(BF16), b_gate[d],
               w_pa[d].astype(BF16), w_pb[d].astype(BF16), w_o[d].astype(BF16),
               g_ffn[d].reshape(1, D_MODEL), w_a[d].astype(BF16), w_b[d].astype(BF16),
               w_conv[d], b_conv[d].reshape(1, D_FF), w_down[d].astype(BF16))
        final = d == depth - 1
        hp, k1, v1, s1, c1 = _layer(hp, pos_p, None, d, wts, g_fin, final, PROMPT_TILES)
        hs, k2, v2, s2, c2 = _layer(hs, pos_s, caches, d, wts, g_fin, final, sample_tiles)
        for lst, val in zip(outs, (k1, v1, s1, c1, k2, v2, s2, c2)):
            lst.append(val)
    stacked = [o[0][None] if depth == 1 else jnp.stack(o) for o in outs]
    return (hp, hs, *stacked)
```

```python
import functools

import jax
import jax.numpy as jnp
from jax import lax
from jax.experimental import pallas as pl
from jax.experimental.pallas import tpu as pltpu

F32 = jnp.float32
BF16 = jnp.bfloat16

D_MODEL = 1024
H_A, DH_A = 8, 64
W_A = H_A * DH_A
H_B, DK_B, DV_B = 4, 128, 256
QK_B = H_B * DK_B
V_B = H_B * DV_B
D_FF = 2816
CONV_W = 3
ROPE_BASE = 10000.0
EPS = 1e-6
LANES = 128
SUBLANES = 8
PAIR = LANES // DH_A

C_QA, C_KA, C_VA = 0, W_A, 2 * W_A
C_QB = 3 * W_A
C_KB = C_QB + QK_B
C_VB = C_KB + QK_B
C_GR = C_VB + V_B
C_GA = C_GR + V_B
C_GB = C_GA + D_MODEL
IN_COLS = C_GB + D_MODEL

VMEM_LIMIT_BYTES = 56 * 1024 * 1024

LOG2E = 1.4426950408889634
STICK_DEAD_LOG2 = -110.0 * LOG2E
STICK_NO_KEYS = -1e30
STICK_MASKED = 1e30


def _dot(a, b):
    return jnp.dot(a, b, preferred_element_type=F32)


def _dot_nt(a, b):
    return lax.dot_general(a, b, (((1,), (1,)), ((), ())), preferred_element_type=F32)


def _rmsnorm(x, g):
    return x * lax.rsqrt(jnp.mean(x * x, axis=-1, keepdims=True) + EPS) * g


def _sigmoid(x):
    return 1.0 / (1.0 + jnp.exp(-x))


def _resident(shape):
    nd = len(shape)
    return pl.BlockSpec(shape, lambda *_: (0,) * nd, pipeline_mode=pl.Buffered(1))


def _in_proj_kernel(x_ref, g_ref, w_ref, cos_ref, sin_ref,
                    qa_ref, ka_ref, va_ref, qb_ref, kb_ref, vb_ref, gr_ref, ga_ref, gb_ref,
                    *kv16_refs):
    h = _rmsnorm(x_ref[0], g_ref[...]).astype(BF16)

    def proj(c0, width):
        return _dot(h, w_ref[:, c0:c0 + width])

    qa_ref[0] = (proj(C_QA, W_A) * (-(DH_A ** -0.5) * LOG2E)).astype(BF16)
    ka = proj(C_KA, W_A)
    va = proj(C_VA, W_A)
    for hh in range(H_A):
        ka_ref[0, hh] = ka[:, hh * DH_A:(hh + 1) * DH_A]
        va_ref[0, hh] = va[:, hh * DH_A:(hh + 1) * DH_A]
    if kv16_refs:
        kv16_refs[0][0] = ka.astype(BF16)
        kv16_refs[1][0] = va.astype(BF16)

    cos = jnp.concatenate([cos_ref[...], cos_ref[...]], axis=-1)
    sin = jnp.concatenate([-sin_ref[...], sin_ref[...]], axis=-1)
    qb = proj(C_QB, QK_B)
    kb = proj(C_KB, QK_B)
    for hh in range(H_B):
        sl = slice(hh * DK_B, (hh + 1) * DK_B)
        q = qb[:, sl]
        k = kb[:, sl]
        qb_ref[0, :, sl] = (q * cos + pltpu.roll(q, DK_B // 2, axis=1) * sin).astype(BF16)
        kr = (k * cos + pltpu.roll(k, DK_B // 2, axis=1) * sin) * (DK_B ** -0.5)
        kb_ref[0, :, sl] = kr.astype(BF16)

    vb_ref[0] = proj(C_VB, V_B).astype(BF16)
    gr_ref[0] = proj(C_GR, V_B).astype(BF16)
    ga_ref[0] = proj(C_GA, D_MODEL).astype(BF16)
    gb_ref[0] = proj(C_GB, D_MODEL).astype(BF16)


def _in_proj(x, g, w_in, cos, sin, tm, token_major_kv):
    nb, l, _ = x.shape
    assert l % tm == 0, (l, tm)
    nt = l // tm
    tok = lambda width: pl.BlockSpec((1, tm, width), lambda b, i: (b, i, 0))
    heads = pl.BlockSpec((1, H_A, tm, DH_A), lambda b, i: (b, 0, i, 0))
    tab = pl.BlockSpec((tm, DK_B // 2), lambda b, i: (i, 0))
    sds = jax.ShapeDtypeStruct
    out_specs = [tok(W_A), heads, heads, tok(QK_B), tok(QK_B), tok(V_B), tok(V_B),
                 tok(D_MODEL), tok(D_MODEL)]
    out_shape = [sds((nb, l, W_A), BF16),
                 sds((nb, H_A, l, DH_A), F32), sds((nb, H_A, l, DH_A), F32),
                 sds((nb, l, QK_B), BF16), sds((nb, l, QK_B), BF16),
                 sds((nb, l, V_B), BF16), sds((nb, l, V_B), BF16),
                 sds((nb, l, D_MODEL), BF16), sds((nb, l, D_MODEL), BF16)]
    if token_major_kv:
        out_specs += [tok(W_A), tok(W_A)]
        out_shape += [sds((nb, l, W_A), BF16), sds((nb, l, W_A), BF16)]
    return pl.pallas_call(
        _in_proj_kernel,
        grid=(nb, nt),
        in_specs=[tok(D_MODEL), _resident((1, D_MODEL)), _resident((D_MODEL, IN_COLS)), tab, tab],
        out_specs=out_specs,
        out_shape=out_shape,
        compiler_params=pltpu.CompilerParams(
            dimension_semantics=("arbitrary", "arbitrary"), vmem_limit_bytes=VMEM_LIMIT_BYTES),
        name="in_proj",
    )(x, g, w_in, cos, sin)


def _upper_ones(n):
    j = lax.broadcasted_iota(jnp.int32, (n, n), 0)
    s = lax.broadcasted_iota(jnp.int32, (n, n), 1)
    return jnp.where(j > s, 1.0, 0.0).astype(BF16)


def _stick_trip(qs, ks, vs, tri, carries, masks, kv_transposed=False):
    score_dot, value_dot = (_dot, _dot_nt) if kv_transposed else (_dot_nt, _dot)
    nzs, lks = [], []
    for q, k, mask in zip(qs, ks, masks):
        nz = score_dot(q, k)
        if mask is not None:
            nz = jnp.where(mask, nz, STICK_MASKED)
        soft = jnp.log(1.0 + jnp.exp2(-jnp.abs(nz))) * LOG2E
        lk = jnp.minimum(nz, 0.0) - soft
        nzs.append(nz)
        lks.append(lk)
    later_all = _dot(jnp.concatenate([lk.astype(BF16) for lk in lks], axis=0), tri)
    results, row0 = [], 0
    for nz, lk, v, carry in zip(nzs, lks, vs, carries):
        later = later_all[row0:row0 + nz.shape[0]]
        row0 += nz.shape[0]
        log_a = (lk - nz) + later
        if carry is not None:
            log_a = log_a + carry
        a = jnp.exp2(log_a)
        results.append((value_dot(a.astype(BF16), v), later[:, 0:1] + lk[:, 0:1]))
    return results


def _any_row_alive(r_ref, first_chain=0):
    m = r_ref[first_chain]
    for c in range(first_chain + 1, r_ref.shape[0]):
        m = jnp.maximum(m, r_ref[c])
    return (jnp.max(m) > STICK_DEAD_LOG2).astype(jnp.int32)


def _stick_prompt_kernel(q_ref, k_ref, v_ref, o_ref, qm_ref, tri_ref, acc_ref, r_ref,
                         *, rows, win, group):
    step = pl.program_id(2)
    tri_ref[...] = _upper_ones(win)
    lane = lax.broadcasted_iota(jnp.int32, (rows, LANES), 1)
    first = lane < DH_A
    for g in range(group):
        qp = q_ref[0, g * rows:(g + 1) * rows, :]
        qm_ref[g, 0:rows] = jnp.where(first, qp, jnp.zeros_like(qp))
        qm_ref[g, rows:PAIR * rows] = jnp.where(first, jnp.zeros_like(qp), qp)

    def load_kv(starts):
        starts = [pl.multiple_of(s, rows) for s in starts]
        return ([k_ref[0, pl.ds(s, win), :] for s in starts],
                [v_ref[0, pl.ds(s, win), :] for s in starts])

    def fold_heads(out):
        return jnp.where(first, out[0:rows], out[rows:PAIR * rows])

    chains = range(group)
    q0s = [(step * group + g) * rows for g in chains]
    starts = [jnp.maximum(q0 - (win - rows), 0) for q0 in q0s]
    t = lax.broadcasted_iota(jnp.int32, (PAIR * rows, win), 0) & (rows - 1)
    s = lax.broadcasted_iota(jnp.int32, (PAIR * rows, win), 1)
    ks, vs = load_kv(starts)
    masks = [s - t < q0 - start for q0, start in zip(q0s, starts)]
    res = _stick_trip([qm_ref[g] for g in chains], ks, vs, tri_ref[...], [None] * group, masks)
    for g, (out, total) in enumerate(res):
        acc_ref[g] = fold_heads(out)
        r_ref[g] = total

    def cond(carry):
        _, alive = carry
        return alive > 0

    def body(carry):
        trip, _ = carry
        limits = [start - win * (trip - 1) for start in starts]
        firsts = [jnp.maximum(limit - win, 0) for limit in limits]
        ks, vs = load_kv(firsts)
        rs = [jnp.where(limit > 0, r_ref[g], STICK_NO_KEYS) for g, limit in zip(chains, limits)]
        masks = [s < limit - lo for limit, lo in zip(limits, firsts)]
        res = _stick_trip([qm_ref[g] for g in chains], ks, vs, tri_ref[...], rs, masks)
        for g, (out, total) in enumerate(res):
            acc_ref[g] += fold_heads(out)
            r_ref[g] = rs[g] + total
        return trip + 1, _any_row_alive(r_ref)

    n_done = min(group, (win - rows) // rows + 1)
    if n_done < group:
        alive = jnp.where(step == 0, _any_row_alive(r_ref, n_done), _any_row_alive(r_ref))
    else:
        alive = jnp.where(step == 0, 0, _any_row_alive(r_ref))
    lax.while_loop(cond, body, (jnp.int32(1), alive))
    for g in chains:
        o_ref[0, g * rows:(g + 1) * rows, :] = acc_ref[g].astype(BF16)


def _stick_prompt(qn, k, v, rows, win, group):
    nb, l, _ = qn.shape
    step_rows = rows * group
    assert l % step_rows == 0 and win % rows == 0 and l >= win, (l, rows, win, group)
    kv = pl.BlockSpec((1, l, LANES), lambda b, p, i: (b, 0, p))
    qo = pl.BlockSpec((1, step_rows, LANES), lambda b, p, i: (b, i, p))
    return pl.pallas_call(
        functools.partial(_stick_prompt_kernel, rows=rows, win=win, group=group),
        grid=(nb, H_A // PAIR, l // step_rows),
        in_specs=[qo, kv, kv],
        out_specs=qo,
        out_shape=jax.ShapeDtypeStruct((nb, l, W_A), BF16),
        scratch_shapes=[pltpu.VMEM((group, PAIR * rows, LANES), BF16),
                        pltpu.VMEM((win, win), BF16),
                        pltpu.VMEM((group, rows, LANES), F32),
                        pltpu.VMEM((group, PAIR * rows, 1), F32)],
        compiler_params=pltpu.CompilerParams(
            dimension_semantics=("arbitrary", "arbitrary", "arbitrary"),
            vmem_limit_bytes=VMEM_LIMIT_BYTES),
        name="stick_prompt",
    )(qn, k, v)


def _stick_sample_kernel(q_ref, kn_ref, vn_ref, kc_ref, vc_ref, o_ref, tri_ref, acc_ref, r_ref,
                         *, blk, heads):
    lq = q_ref.shape[1]
    n_cache = kc_ref.shape[4] // blk
    tri_ref[...] = _upper_ones(blk)
    chains = range(heads)

    def qs():
        return [q_ref[0, :, hh * DH_A:(hh + 1) * DH_A] for hh in chains]

    t = lax.broadcasted_iota(jnp.int32, (lq, lq), 0)
    s = lax.broadcasted_iota(jnp.int32, (lq, lq), 1)
    res = _stick_trip(qs(), [kn_ref[0, hh].astype(BF16) for hh in chains],
                      [vn_ref[0, hh].astype(BF16) for hh in chains], _upper_ones(lq),
                      [None] * heads, [s < t] * heads)
    for hh, (out, total) in enumerate(res):
        acc_ref[hh] = out
        r_ref[hh] = total

    def cond(carry):
        kb, alive = carry
        return jnp.logical_and(kb >= 0, alive > 0)

    def body(carry):
        kb, _ = carry
        start = pl.multiple_of(kb * blk, blk)
        ks = [kc_ref[0, 0, hh, :, pl.ds(start, blk)].astype(BF16) for hh in chains]
        vs = [vc_ref[0, 0, hh, :, pl.ds(start, blk)].astype(BF16) for hh in chains]
        rs = [r_ref[hh] for hh in chains]
        res = _stick_trip(qs(), ks, vs, tri_ref[...], rs, [None] * heads, kv_transposed=True)
        for hh, (out, total) in enumerate(res):
            acc_ref[hh] += out
            r_ref[hh] = rs[hh] + total
        return kb - 1, _any_row_alive(r_ref)

    lax.while_loop(cond, body, (jnp.int32(n_cache - 1), _any_row_alive(r_ref)))
    for hh in chains:
        o_ref[0, :, hh * DH_A:(hh + 1) * DH_A] = acc_ref[hh].astype(BF16)


def _stick_sample(qn, k_new, v_new, k_cache_t, v_cache_t, depth_idx, blk, heads):
    nb, lq, _ = qn.shape
    past = k_cache_t.shape[4]
    new = pl.BlockSpec((1, heads, lq, DH_A), lambda b, h: (b, h, 0, 0))
    old = pl.BlockSpec((1, 1, heads, DH_A, past), lambda b, h: (depth_idx, b, h, 0, 0))
    qo = pl.BlockSpec((1, lq, heads * DH_A), lambda b, h: (b, 0, h))
    return pl.pallas_call(
        functools.partial(_stick_sample_kernel, blk=blk, heads=heads),
        grid=(nb, H_A // heads),
        in_specs=[qo, new, new, old, old],
        out_specs=qo,
        out_shape=jax.ShapeDtypeStruct((nb, lq, W_A), BF16),
        scratch_shapes=[pltpu.VMEM((blk, blk), BF16),
                        pltpu.VMEM((heads, lq, DH_A), F32),
                        pltpu.VMEM((heads, lq, 1), F32)],
        compiler_params=pltpu.CompilerParams(
            dimension_semantics=("arbitrary", "arbitrary"), vmem_limit_bytes=VMEM_LIMIT_BYTES),
        name="stick_sample",
    )(qn, k_new, v_new, k_cache_t, v_cache_t)


def _retention_tile(lg_ref, q_ref, k_ref, v_ref, gr_ref, state_ref, decay_ref):
    chunk = q_ref.shape[1]
    idx = lax.broadcasted_iota(jnp.int32, (chunk, 1), 0).astype(F32)
    outs = []
    for hh in range(H_B):
        lg = lg_ref[hh]
        q = q_ref[0, :, hh * DK_B:(hh + 1) * DK_B]
        k = k_ref[0, :, hh * DK_B:(hh + 1) * DK_B]
        v = v_ref[0, :, hh * DV_B:(hh + 1) * DV_B]
        state = state_ref[hh]

        scores = _dot_nt(q, k) * decay_ref[hh]
        o = _dot(scores.astype(BF16), v) + jnp.exp((idx + 1.0) * lg) * _dot(q, state.astype(BF16))
        k_dec = (k.astype(F32) * jnp.exp((chunk - 1.0 - idx) * lg)).T.astype(BF16)
        state_ref[hh] = jnp.exp(chunk * lg) * state + _dot(k_dec, v)

        o = o * lax.rsqrt(jnp.mean(o * o, axis=-1, keepdims=True) + EPS)
        g = gr_ref[0, :, hh * DV_B:(hh + 1) * DV_B].astype(F32)
        outs.append((o * (g * _sigmoid(g))).astype(BF16))
    return jnp.concatenate(outs, axis=-1)


def _mix_ffn_kernel(lg_ref, x_ref, oa_ref, q_ref, k_ref, v_ref, gr_ref,
                    qn_ref, kn_ref, vn_ref, grn_ref, ga_ref, gb_ref,
                    s0_ref, cbuf_ref, bg_ref, wpa_ref, wpb_ref, wo_ref, gffn_ref, wa_ref, wb_ref,
                    wc_ref, bc_ref, wd_ref, gfin_ref, y_ref, s1_ref, cnew_ref,
                    state_ref, decay_ref, carry_ref, ob_ref, *, final):
    tm = x_ref.shape[1]
    c = pl.program_id(1)

    @pl.when(c == 0)
    def _():
        carry_ref[...] = cbuf_ref[0]
        i = lax.broadcasted_iota(jnp.int32, (tm, tm), 0)
        j = lax.broadcasted_iota(jnp.int32, (tm, tm), 1)
        diff = (i - j).astype(F32)
        for hh in range(H_B):
            state_ref[hh] = s0_ref[0, hh]
            decay_ref[hh] = jnp.where(diff >= 0, jnp.exp(jnp.maximum(diff, 0.0) * lg_ref[hh]), 0.0)
        ob_ref[...] = _retention_tile(lg_ref, q_ref, k_ref, v_ref, gr_ref, state_ref, decay_ref)

    @pl.when(c == pl.num_programs(1) - 1)
    def _():
        for hh in range(H_B):
            s1_ref[0, hh] = state_ref[hh]

    gate_a = _sigmoid(ga_ref[0].astype(F32) + bg_ref[0:1, :])
    gate_b = _sigmoid(gb_ref[0].astype(F32) + bg_ref[1:2, :])
    mix = gate_a * _dot(oa_ref[0], wpa_ref[...]) + gate_b * _dot(ob_ref[...], wpb_ref[...])
    x1 = x_ref[0] + _dot(mix.astype(BF16), wo_ref[...])

    hn = _rmsnorm(x1, gffn_ref[...]).astype(BF16)
    a = _dot(hn, wa_ref[...])
    up = _dot(hn, wb_ref[...])
    ob_ref[...] = _retention_tile(lg_ref, qn_ref, kn_ref, vn_ref, grn_ref, state_ref, decay_ref)
    prev2 = carry_ref[0:1, :]
    prev1 = carry_ref[1:2, :]
    row = lax.broadcasted_iota(jnp.int32, (SUBLANES, 1), 0)
    a_m1 = pltpu.roll(a, 1, axis=0)
    a_m2 = pltpu.roll(a, 2, axis=0)
    top1 = jnp.where(row == 0, prev1, a_m1[:SUBLANES])
    top2 = jnp.where(row == 0, prev2, jnp.where(row == 1, prev1, a_m2[:SUBLANES]))
    a_m1 = jnp.concatenate([top1, a_m1[SUBLANES:]], axis=0)
    a_m2 = jnp.concatenate([top2, a_m2[SUBLANES:]], axis=0)
    conv = wc_ref[0:1, :] * a_m2 + wc_ref[1:2, :] * a_m1 + wc_ref[2:3, :] * a + bc_ref[...]
    hid = conv * _sigmoid(conv) * up
    x2 = x1 + _dot(hid.astype(BF16), wd_ref[...])

    y_ref[0] = _rmsnorm(x2, gfin_ref[...]) if final else x2
    tail = a[tm - (CONV_W - 1):, :]
    carry_ref[...] = tail
    cnew_ref[0] = tail


def _mix_ffn(log_gamma, x, oa, qb, kb, vb, gr, ga, gb, s0, s0_lead, conv_buf, wts, g_final, tm,
             final):
    b_gate, w_pa, w_pb, w_o, g_ffn, w_a, w_b, w_conv, b_conv, w_down = wts
    nb, l, _ = x.shape
    assert l % tm == 0 and tm >= CONV_W - 1, (l, tm)
    tok = lambda width: pl.BlockSpec((1, tm, width), lambda b, i: (b, i, 0))
    cb = pl.BlockSpec((1, CONV_W - 1, D_FF), lambda b, i: (b, 0, 0))
    st = pl.BlockSpec((1, H_B, DK_B, DV_B), lambda b, i: (b, 0, 0, 0))
    st_in = pl.BlockSpec((None,) * len(s0_lead) + (1, H_B, DK_B, DV_B),
                         lambda b, i: (*s0_lead, b, 0, 0, 0))
    consts = [b_gate, w_pa, w_pb, w_o, g_ffn, w_a, w_b, w_conv, b_conv, w_down, g_final]
    last = l // tm - 1
    ahead = lambda width: pl.BlockSpec((1, tm, width),
                                       lambda b, i: (b, jnp.minimum(i + 1, last), 0))
    return pl.pallas_call(
        functools.partial(_mix_ffn_kernel, final=final),
        grid=(nb, l // tm),
        in_specs=[pl.BlockSpec(memory_space=pltpu.SMEM), tok(D_MODEL), tok(W_A),
                  tok(QK_B), tok(QK_B), tok(V_B), tok(V_B),
                  ahead(QK_B), ahead(QK_B), ahead(V_B), ahead(V_B),
                  tok(D_MODEL), tok(D_MODEL), st_in, cb]
                 + [_resident(c.shape) for c in consts],
        out_specs=[tok(D_MODEL), st, cb],
        out_shape=[jax.ShapeDtypeStruct((nb, l, D_MODEL), F32),
                   jax.ShapeDtypeStruct((nb, H_B, DK_B, DV_B), F32),
                   jax.ShapeDtypeStruct((nb, CONV_W - 1, D_FF), F32)],
        scratch_shapes=[pltpu.VMEM((H_B, DK_B, DV_B), F32), pltpu.VMEM((H_B, tm, tm), F32),
                        pltpu.VMEM((CONV_W - 1, D_FF), F32), pltpu.VMEM((tm, V_B), BF16)],
        compiler_params=pltpu.CompilerParams(
            dimension_semantics=("arbitrary", "arbitrary"), vmem_limit_bytes=VMEM_LIMIT_BYTES),
        name="mix_ffn",
    )(log_gamma, x, oa, qb, kb, vb, gr, qb, kb, vb, gr, ga, gb, s0, conv_buf, *consts)


def _rope_tables(pos):
    half = DK_B // 2
    inv_freq = ROPE_BASE ** (-jnp.arange(half, dtype=F32) / half)
    ang = pos.astype(F32)[:, None] * inv_freq[None, :]
    return jnp.cos(ang), jnp.sin(ang)


def _layer(x, pos, caches, depth_idx, wts, g_final, final, tiles):
    g_mix, w_in, *rest = wts
    nb, l, _ = x.shape
    cos, sin = _rope_tables(pos)
    fresh = caches is None
    if fresh:
        qa, ka, va, qb, kb, vb, gr, ga, gb, *kv16 = _in_proj(
            x, g_mix, w_in, cos, sin, tiles["in"], token_major_kv=True)
    else:
        flat = _in_proj(x.reshape(1, nb * l, D_MODEL), g_mix, w_in, jnp.tile(cos, (nb, 1)),
                        jnp.tile(sin, (nb, 1)), nb * l, token_major_kv=False)
        unflat = lambda t: t.reshape(nb, l, t.shape[-1])
        heads = lambda t: t.reshape(H_A, nb, l, DH_A).transpose(1, 0, 2, 3)
        qa, ka, va, qb, kb, vb, gr, ga, gb = (
            unflat(flat[0]), heads(flat[1]), heads(flat[2]), *map(unflat, flat[3:]))

    log_gamma = jnp.log1p(-jnp.exp2(-5.0 - jnp.arange(H_B, dtype=F32)))
    if fresh:
        o_a = _stick_prompt(qa, kv16[0], kv16[1], tiles["stick_rows"], tiles["stick"],
                            tiles["stick_group"])
        ret_state, lead = jnp.zeros((nb, H_B, DK_B, DV_B), F32), ()
        conv_buf = jnp.zeros((nb, CONV_W - 1, D_FF), F32)
    else:
        cache_k, cache_v, ret_state, state_conv = caches
        o_a = _stick_sample(qa, ka, va, cache_k, cache_v, depth_idx, tiles["stick"],
                            tiles["stick_heads"])
        lead = (depth_idx,)
        conv_buf = state_conv[depth_idx]
    y, s_new, c_new = _mix_ffn(log_gamma, x, o_a, qb, kb, vb, gr, ga, gb, ret_state, lead,
                               conv_buf, rest, g_final, tiles["out"], final)
    return y, ka, va, s_new, c_new


PROMPT_TILES = {"in": 512, "stick": 256, "stick_rows": 64, "stick_group": 32, "out": 256}


def kernel(x_prompt, x_sample, cache_k_sb, cache_v_sb, state_ret, state_conv, g_mix, w_in, b_gate,
           w_pa, w_pb, w_o, g_ffn, w_a, w_b, w_conv, b_conv, w_down, g_final):
    depth = w_in.shape[0]
    past = cache_k_sb.shape[3]
    dec = x_sample.shape[1]
    pos_p = jnp.arange(x_prompt.shape[1], dtype=jnp.int32)
    pos_s = past + jnp.arange(dec, dtype=jnp.int32)
    sample_tiles = {"in": dec, "stick": 256, "stick_heads": 4, "out": dec}
    g_fin = g_final.reshape(1, D_MODEL)
    caches = (jnp.swapaxes(cache_k_sb, 3, 4), jnp.swapaxes(cache_v_sb, 3, 4), state_ret,
              state_conv)

    hp, hs = x_prompt, x_sample
    outs = [[] for _ in range(8)]
    for d in range(depth):
        wts = (g_mix[d].reshape(1, D_MODEL), w_in[d].astype(BF16), b_gate[d],
               w_pa[d].astype(BF16), w_pb[d].astype(BF16), w_o[d].astype(BF16),
               g_ffn[d].reshape(1, D_MODEL), w_a[d].astype(BF16), w_b[d].astype(BF16),
               w_conv[d], b_conv[d].reshape(1, D_FF), w_down[d].astype(BF16))
        final = d == depth - 1
        hp, k1, v1, s1, c1 = _layer(hp, pos_p, None, d, wts, g_fin, final, PROMPT_TILES)
        hs, k2, v2, s2, c2 = _layer(hs, pos_s, caches, d, wts, g_fin, final, sample_tiles)
        for lst, val in zip(outs, (k1, v1, s1, c1, k2, v2, s2, c2)):
            lst.append(val)
    stacked = [o[0][None] if depth == 1 else jnp.stack(o) for o in outs]
    return (hp, hs, *stacked)
```
